```python
import math
import jax, jax.numpy as jnp
from jax import lax
import numpy as np

D_MODEL = 4096
BATCH = 4
SEQ = 4096
DEPTH = 1

MIX_WIDTH = D_MODEL
CONV_CH = MIX_WIDTH // 2
CONV_WIDTH = 31
CONV_PAD = CONV_WIDTH // 2
ATTN_HEADS = 8
ATTN_VDIM = (MIX_WIDTH - CONV_CH) // ATTN_HEADS
ATTN_QKDIM = ATTN_VDIM // 2
ROPE_THETA = 10000.0
Q_BLOCK = 128
N_EXPERTS = 16
CAPACITY_FACTOR = 2
D_FF_EXPERT = D_MODEL
RMS_EPS = 1e-6
LN_EPS = 1e-5
SUBLN_EPS = 1e-5

QK_COLS = ATTN_HEADS * 2 * ATTN_QKDIM
V_COLS = ATTN_HEADS * ATTN_VDIM
IN_COLS = 2 * CONV_CH + 2 * QK_COLS + V_COLS

kernel_name = "hybrid_conformer_diffattn_ecmoe_block"


def rmsnorm(x, g, eps=RMS_EPS):
    xf = x.astype(jnp.float32)
    y = xf * lax.rsqrt(jnp.mean(xf * xf, axis=-1, keepdims=True) + eps)
    return (y * g.astype(jnp.float32)).astype(x.dtype)


def layernorm(x, g, b, eps=LN_EPS):
    xf = x.astype(jnp.float32)
    mu = jnp.mean(xf, axis=-1, keepdims=True)
    xc = xf - mu
    y = xc * lax.rsqrt(jnp.mean(xc * xc, axis=-1, keepdims=True) + eps)
    return (y * g.astype(jnp.float32) + b.astype(jnp.float32)).astype(x.dtype)


def rope_tables(positions, dim):
    inv_freq = ROPE_THETA ** (-jnp.arange(0, dim, 2, dtype=jnp.float32) / dim)
    ang = positions.astype(jnp.float32)[..., None] * inv_freq
    return jnp.cos(ang)[:, None, None], jnp.sin(ang)[:, None, None]


def apply_rope(t, cos, sin):
    tf = t.astype(jnp.float32)
    half = tf.shape[-1] // 2
    t1, t2 = tf[..., :half], tf[..., half:]
    out = jnp.concatenate([t1 * cos - t2 * sin, t2 * cos + t1 * sin], axis=-1)
    return out.astype(t.dtype)


def lambda_init_fn(layer_idx):
    return 0.8 - 0.6 * math.exp(-0.3 * layer_idx)


def conformer_conv_group(u_a, u_g, conv_w, conv_b, ln_g, ln_b):
    glu = u_a * jax.nn.sigmoid(u_g)
    kern = conv_w[:, None, :].astype(glu.dtype)
    y = lax.conv_general_dilated(
        glu, kern, window_strides=(1,), padding=[(CONV_PAD, CONV_PAD)],
        dimension_numbers=("NWC", "WIO", "NWC"), feature_group_count=CONV_CH)
    y = y + conv_b.astype(y.dtype)
    y = layernorm(y, ln_g, ln_b)
    return jax.nn.silu(y)


def diff_attention_group(q, k, v, cos, sin, lq1, lk1, lq2, lk2, subln_g, lam_init):
    B, S, _ = q.shape
    q = q.reshape(B, S, ATTN_HEADS, 2, ATTN_QKDIM).transpose(0, 2, 3, 1, 4)
    k = k.reshape(B, S, ATTN_HEADS, 2, ATTN_QKDIM).transpose(0, 2, 3, 1, 4)
    v = v.reshape(B, S, ATTN_HEADS, ATTN_VDIM).transpose(0, 2, 1, 3)
    q = apply_rope(q, cos, sin) * (ATTN_QKDIM ** -0.5)
    k = apply_rope(k, cos, sin)

    lam = (jnp.exp(jnp.sum(lq1.astype(jnp.float32) * lk1.astype(jnp.float32)))
           - jnp.exp(jnp.sum(lq2.astype(jnp.float32) * lk2.astype(jnp.float32)))
           + lam_init)

    n_blk = S // Q_BLOCK
    q_blocks = q.reshape(B, ATTN_HEADS, 2, n_blk, Q_BLOCK, ATTN_QKDIM).transpose(3, 0, 1, 2, 4, 5)

    def one_block(qb):
        s = jnp.einsum("bhtqd,bhtkd->bhtqk", qb, k).astype(jnp.float32)
        p = jax.nn.softmax(s, axis=-1)
        a = p[:, :, 0] - lam * p[:, :, 1]
        return jnp.einsum("bhqk,bhkv->bhqv", a.astype(v.dtype), v)

    o = lax.map(one_block, q_blocks)
    o = o.transpose(1, 2, 0, 3, 4).reshape(B, ATTN_HEADS, S, ATTN_VDIM)
    o = rmsnorm(o, subln_g, SUBLN_EPS) * (1.0 - lam_init)
    return o.transpose(0, 2, 1, 3).reshape(B, S, V_COLS)


def expert_choice_moe(xn, w_router, w_gate, w_up, w_down):
    B, S, _ = xn.shape
    cap = CAPACITY_FACTOR * S // N_EXPERTS
    logits = jnp.einsum("bsd,de->bse", xn, w_router).astype(jnp.float32)
    aff = jax.nn.softmax(logits, axis=-1)
    gates, idx = lax.top_k(aff.transpose(0, 2, 1), cap)
    bix = jnp.arange(B)[:, None, None]
    xs = xn[bix, idx]
    hg = jnp.einsum("becd,edf->becf", xs, w_gate)
    hu = jnp.einsum("becd,edf->becf", xs, w_up)
    out = jnp.einsum("becf,efd->becd", jax.nn.silu(hg) * hu, w_down)
    out = out * gates[..., None].astype(out.dtype)
    return jnp.zeros_like(xn).at[bix, idx].add(out.astype(xn.dtype))


def setup_inputs(seed: int = 0) -> dict:
    key = jax.random.key(seed)
    ks = jax.random.split(key, 20)
    f32 = jnp.float32
    nrm = lambda k, shape, scale: jax.random.normal(k, shape, f32) * scale
    x = jax.random.normal(ks[0], (BATCH, SEQ, D_MODEL), f32)
    positions = jnp.broadcast_to(jnp.arange(SEQ, dtype=jnp.int32), (BATCH, SEQ))
    return {
        "x": x,
        "positions": positions,
        "norm1_g": 1.0 + nrm(ks[1], (DEPTH, D_MODEL), 0.02),
        "w_in": nrm(ks[2], (DEPTH, D_MODEL, IN_COLS), D_MODEL ** -0.5),
        "conv_w": nrm(ks[3], (DEPTH, CONV_WIDTH, CONV_CH), CONV_WIDTH ** -0.5),
        "conv_b": nrm(ks[4], (DEPTH, CONV_CH), 0.02),
        "conv_ln_g": 1.0 + nrm(ks[5], (DEPTH, CONV_CH), 0.02),
        "conv_ln_b": nrm(ks[6], (DEPTH, CONV_CH), 0.02),
        "lq1": nrm(ks[7], (DEPTH, ATTN_QKDIM), 0.1),
        "lk1": nrm(ks[8], (DEPTH, ATTN_QKDIM), 0.1),
        "lq2": nrm(ks[9], (DEPTH, ATTN_QKDIM), 0.1),
        "lk2": nrm(ks[10], (DEPTH, ATTN_QKDIM), 0.1),
        "subln_g": 1.0 + nrm(ks[11], (DEPTH, ATTN_VDIM), 0.02),
        "w_out": nrm(ks[12], (DEPTH, MIX_WIDTH, D_MODEL), MIX_WIDTH ** -0.5),
        "norm2_g": 1.0 + nrm(ks[13], (DEPTH, D_MODEL), 0.02),
        "w_router": nrm(ks[14], (DEPTH, D_MODEL, N_EXPERTS), D_MODEL ** -0.5),
        "w_gate": nrm(ks[15], (DEPTH, N_EXPERTS, D_MODEL, D_FF_EXPERT), D_MODEL ** -0.5),
        "w_up": nrm(ks[16], (DEPTH, N_EXPERTS, D_MODEL, D_FF_EXPERT), D_MODEL ** -0.5),
        "w_down": nrm(ks[17], (DEPTH, N_EXPERTS, D_FF_EXPERT, D_MODEL), D_FF_EXPERT ** -0.5),
        "final_g": 1.0 + nrm(ks[18], (D_MODEL,), 0.02),
    }


def reference(x, positions, norm1_g, w_in, conv_w, conv_b, conv_ln_g, conv_ln_b,
              lq1, lk1, lq2, lk2, subln_g, w_out, norm2_g, w_router, w_gate, w_up,
              w_down, final_g):
    cos, sin = rope_tables(positions, ATTN_QKDIM)
    c0 = CONV_CH
    c1 = 2 * CONV_CH
    c2 = c1 + QK_COLS
    c3 = c2 + QK_COLS
    for l in range(DEPTH):
        xn = rmsnorm(x, norm1_g[l])
        u = jnp.einsum("bsd,dc->bsc", xn, w_in[l])
        conv_out = conformer_conv_group(u[..., :c0], u[..., c0:c1],
                                        conv_w[l], conv_b[l], conv_ln_g[l], conv_ln_b[l])
        attn_out = diff_attention_group(u[..., c1:c2], u[..., c2:c3], u[..., c3:],
                                        cos, sin, lq1[l], lk1[l], lq2[l], lk2[l],
                                        subln_g[l], lambda_init_fn(l))
        mixed = jnp.concatenate([conv_out, attn_out.astype(conv_out.dtype)], axis=-1)
        x = x + jnp.einsum("bsc,cd->bsd", mixed, w_out[l])
        hn = rmsnorm(x, norm2_g[l])
        x = x + expert_choice_moe(hn, w_router[l], w_gate[l], w_up[l], w_down[l])
    return rmsnorm(x, final_g)
```

```python
import functools
import math

import jax
import jax.numpy as jnp
from jax import lax
from jax.experimental import pallas as pl
from jax.experimental.pallas import tpu as pltpu

F32 = jnp.float32
BF16 = jnp.bfloat16

V7X_LANES = 128
V7X_SUBLANES = 8
V7X_VMEM_LIMIT_BYTES = 56 * 1024 * 1024

ROPE_THETA = 10000.0
RMS_EPS = 1e-6
LN_EPS = 1e-5
SUBLN_EPS = 1e-5
CAPACITY_FACTOR = 2
CONV_HALO = 16


def _params(*sem):
    return pltpu.CompilerParams(dimension_semantics=sem, vmem_limit_bytes=V7X_VMEM_LIMIT_BYTES)


def _tile(n, want):
    t = min(n, want)
    assert n % t == 0, (n, want)
    return t


def _rmsnorm_kernel(x_ref, g_ref, o_ref, *, eps):
    x = x_ref[...]
    r = lax.rsqrt(jnp.mean(x * x, axis=-1, keepdims=True) + eps)
    o_ref[...] = ((x * r) * g_ref[...]).astype(o_ref.dtype)


def _rmsnorm(x, g, eps, out_dtype):
    t, d = x.shape
    tm = _tile(t, 256)
    return pl.pallas_call(
        functools.partial(_rmsnorm_kernel, eps=eps),
        grid=(t // tm,),
        in_specs=[pl.BlockSpec((tm, d), lambda i: (i, 0)), pl.BlockSpec((1, d), lambda i: (0, 0))],
        out_specs=pl.BlockSpec((tm, d), lambda i: (i, 0)),
        out_shape=jax.ShapeDtypeStruct((t, d), out_dtype),
        compiler_params=_params("parallel"),
        name="rmsnorm",
    )(x, g.reshape(1, d))


def _glu_mm_kernel(a_ref, wa_ref, wb_ref, o_ref, *, swiglu):
    a = a_ref[0]
    ua = jnp.dot(a, wa_ref[0], preferred_element_type=F32)
    ub = jnp.dot(a, wb_ref[0], preferred_element_type=F32)
    if swiglu:
        out = (ua * jax.nn.sigmoid(ua)) * ub
    else:
        out = ua * jax.nn.sigmoid(ub)
    o_ref[0] = out.astype(o_ref.dtype)


def _glu_mm(a, wa, wb, n, a_col, b_col, swiglu, out_dtype, tm=1024, tn=512):
    g, m, k = a.shape
    tm = _tile(m, tm)
    tn = _tile(n, tn)
    assert a_col % tn == 0 and b_col % tn == 0
    ao, bo = a_col // tn, b_col // tn
    return pl.pallas_call(
        functools.partial(_glu_mm_kernel, swiglu=swiglu),
        grid=(g, n // tn, m // tm),
        in_specs=[
            pl.BlockSpec((1, tm, k), lambda e, j, i: (e, i, 0)),
            pl.BlockSpec((1, k, tn), lambda e, j, i: (e, 0, j + ao)),
            pl.BlockSpec((1, k, tn), lambda e, j, i: (e, 0, j + bo)),
        ],
        out_specs=pl.BlockSpec((1, tm, tn), lambda e, j, i: (e, i, j)),
        out_shape=jax.ShapeDtypeStruct((g, m, n), out_dtype),
        compiler_params=_params("parallel", "parallel", "parallel"),
        name="glu_matmul",
    )(a, wa, wb)


def _rope_mm_kernel(a_ref, w_ref, cos_ref, sin_ref, o_ref, *, n_q_tiles, q_scale, dh):
    u = jnp.dot(a_ref[...], w_ref[...], preferred_element_type=F32)
    cos = cos_ref[...]
    sin = sin_ref[...]
    scale = jnp.where(pl.program_id(0) < n_q_tiles, q_scale, 1.0).astype(F32)
    tn = u.shape[1]
    for c in range(tn // dh):
        t = u[:, c * dh:(c + 1) * dh]
        r = t * cos + pltpu.roll(t, dh // 2, axis=1) * sin
        o_ref[:, c * dh:(c + 1) * dh] = (r * scale).astype(o_ref.dtype)


def _rope_mm(a, w, col, n, n_q, cos, sin, q_scale, dh, tm=1024, tn=512):
    m, k = a.shape
    tm = _tile(m, tm)
    tn = _tile(n_q, tn)
    assert col % tn == 0 and n % tn == 0 and tn % dh == 0
    co = col // tn
    return pl.pallas_call(
        functools.partial(_rope_mm_kernel, n_q_tiles=n_q // tn, q_scale=q_scale, dh=dh),
        grid=(n // tn, m // tm),
        in_specs=[
            pl.BlockSpec((tm, k), lambda j, i: (i, 0)),
            pl.BlockSpec((k, tn), lambda j, i: (0, j + co)),
            pl.BlockSpec((tm, dh), lambda j, i: (i, 0)),
            pl.BlockSpec((tm, dh), lambda j, i: (i, 0)),
        ],
        out_specs=pl.BlockSpec((tm, tn), lambda j, i: (i, j)),
        out_shape=jax.ShapeDtypeStruct((m, n), BF16),
        compiler_params=_params("parallel", "parallel"),
        name="rope_matmul",
    )(a, w, cos, sin)


def _mm_kernel(a_ref, w_ref, o_ref):
    o_ref[...] = jnp.dot(a_ref[...], w_ref[...], preferred_element_type=F32).astype(o_ref.dtype)


def _mm(a, w, col, n, out_dtype, tm=1024, tn=512):
    m, k = a.shape
    tm = _tile(m, tm)
    tn = _tile(n, tn)
    assert col % tn == 0
    co = col // tn
    return pl.pallas_call(
        _mm_kernel,
        grid=(n // tn, m // tm),
        in_specs=[pl.BlockSpec((tm, k), lambda j, i: (i, 0)), pl.BlockSpec((k, tn), lambda j, i: (0, j + co))],
        out_specs=pl.BlockSpec((tm, tn), lambda j, i: (i, j)),
        out_shape=jax.ShapeDtypeStruct((m, n), out_dtype),
        compiler_params=_params("parallel", "parallel"),
        name="matmul",
    )(a, w)


def _out_mm_kernel(a1_ref, a2_ref, w1_ref, w2_ref, x_ref, o_ref):
    acc = jnp.dot(a1_ref[...], w1_ref[...], preferred_element_type=F32)
    acc = acc + jnp.dot(a2_ref[...], w2_ref[...], preferred_element_type=F32)
    o_ref[...] = x_ref[...] + acc


def _out_mm(a1, a2, w, x, tm=1024, tn=512):
    m, k1 = a1.shape
    k2 = a2.shape[1]
    n = w.shape[1]
    assert k1 == k2 and w.shape[0] == k1 + k2
    tm = _tile(m, tm)
    tn = _tile(n, tn)
    return pl.pallas_call(
        _out_mm_kernel,
        grid=(n // tn, m // tm),
        in_specs=[
            pl.BlockSpec((tm, k1), lambda j, i: (i, 0)),
            pl.BlockSpec((tm, k2), lambda j, i: (i, 0)),
            pl.BlockSpec((k1, tn), lambda j, i: (0, j)),
            pl.BlockSpec((k2, tn), lambda j, i: (1, j)),
            pl.BlockSpec((tm, tn), lambda j, i: (i, j)),
        ],
        out_specs=pl.BlockSpec((tm, tn), lambda j, i: (i, j)),
        out_shape=jax.ShapeDtypeStruct((m, n), F32),
        compiler_params=_params("parallel", "parallel"),
        name="out_matmul",
    )(a1, a2, w, w, x)


def _down_mm_kernel(a_ref, w_ref, gate_ref, o_ref):
    acc = jnp.dot(a_ref[0], w_ref[0], preferred_element_type=F32)
    o_ref[0] = (acc * gate_ref[0]).astype(o_ref.dtype)


def _down_mm(a, w, gates, out_dtype, tm=1024, tn=512):
    g, m, k = a.shape
    n = w.shape[2]
    tm = _tile(m, tm)
    tn = _tile(n, tn)
    return pl.pallas_call(
        _down_mm_kernel,
        grid=(g, n // tn, m // tm),
        in_specs=[
            pl.BlockSpec((1, tm, k), lambda e, j, i: (e, i, 0)),
            pl.BlockSpec((1, k, tn), lambda e, j, i: (e, 0, j)),
            pl.BlockSpec((1, tm, 1), lambda e, j, i: (e, i, 0)),
        ],
        out_specs=pl.BlockSpec((1, tm, tn), lambda e, j, i: (e, i, j)),
        out_shape=jax.ShapeDtypeStruct((g, m, n), out_dtype),
        compiler_params=_params("parallel", "parallel", "parallel"),
        name="down_matmul",
    )(a, w, gates)


def _conv_kernel(prev_ref, cur_ref, next_ref, cw_ref, cb_ref, lg_ref, lb_ref, o_ref,
                 win_ref, sh_ref, y_ref, *, ts, width, rc):
    i = pl.program_id(1)
    n = pl.num_programs(1)
    halo = CONV_HALO
    pad = width // 2
    c = cur_ref.shape[2]
    win_ref[0:halo, :] = jnp.where(i > 0, prev_ref[0], 0.0)
    win_ref[halo:halo + ts, :] = cur_ref[0]
    win_ref[halo + ts:halo + ts + halo, :] = jnp.where(i < n - 1, next_ref[0], 0.0)
    span = ts + 3 * V7X_SUBLANES
    for r in range(V7X_SUBLANES):
        sh_ref[r] = win_ref[r:r + span, :]

    def chunk_body(rb, carry):
        r0 = pl.multiple_of(rb * rc, rc)
        for cc in range(c // V7X_LANES):
            lanes = slice(cc * V7X_LANES, (cc + 1) * V7X_LANES)
            acc = jnp.zeros((rc, V7X_LANES), F32) + cb_ref[:, lanes]
            for w in range(width):
                off = halo - pad + w
                rows = pl.ds(pl.multiple_of(r0 + V7X_SUBLANES * (off // V7X_SUBLANES), V7X_SUBLANES), rc)
                acc = acc + sh_ref[off % V7X_SUBLANES, rows, lanes] * cw_ref[w:w + 1, lanes]
            y_ref[pl.ds(r0, rc), lanes] = acc
        return carry

    lax.fori_loop(0, ts // rc, chunk_body, 0)
    y = y_ref[...]
    mu = jnp.mean(y, axis=-1, keepdims=True)
    yc = y - mu
    var = jnp.mean(yc * yc, axis=-1, keepdims=True)
    z = (yc * lax.rsqrt(var + LN_EPS)) * lg_ref[...] + lb_ref[...]
    o_ref[0] = (z * jax.nn.sigmoid(z)).astype(o_ref.dtype)


def _conv(glu, conv_w, conv_b, ln_g, ln_b, ts=256, rc=64):
    b, s, c = glu.shape
    width = conv_w.shape[0]
    halo = CONV_HALO
    assert width // 2 < halo and c % V7X_LANES == 0
    ts = _tile(s, ts)
    rc = _tile(ts, rc)
    nh = ts // halo
    last = s // halo - 1
    return pl.pallas_call(
        functools.partial(_conv_kernel, ts=ts, width=width, rc=rc),
        grid=(b, s // ts),
        in_specs=[
            pl.BlockSpec((1, halo, c), lambda bi, i: (bi, jnp.maximum(i * nh - 1, 0), 0)),
            pl.BlockSpec((1, ts, c), lambda bi, i: (bi, i, 0)),
            pl.BlockSpec((1, halo, c), lambda bi, i: (bi, jnp.minimum((i + 1) * nh, last), 0)),
            pl.BlockSpec((width, c), lambda bi, i: (0, 0)),
            pl.BlockSpec((1, c), lambda bi, i: (0, 0)),
            pl.BlockSpec((1, c), lambda bi, i: (0, 0)),
            pl.BlockSpec((1, c), lambda bi, i: (0, 0)),
        ],
        out_specs=pl.BlockSpec((1, ts, c), lambda bi, i: (bi, i, 0)),
        out_shape=jax.ShapeDtypeStruct((b, s, c), BF16),
        scratch_shapes=[
            pltpu.VMEM((ts + 2 * halo, c), F32),
            pltpu.VMEM((V7X_SUBLANES, ts + 3 * V7X_SUBLANES, c), F32),
            pltpu.VMEM((ts, c), F32),
        ],
        compiler_params=_params("parallel", "parallel"),
        name="conformer_conv",
    )(glu, glu, glu, conv_w, conv_b.reshape(1, c), ln_g.reshape(1, c), ln_b.reshape(1, c))


def _attn_kernel(q_ref, k_ref, v_ref, lq1_ref, lk1_ref, lq2_ref, lk2_ref, g_ref, o_ref, *, dh, lam_init):
    lam = (jnp.exp(jnp.sum(lq1_ref[...] * lk1_ref[...], keepdims=True))
           - jnp.exp(jnp.sum(lq2_ref[...] * lk2_ref[...], keepdims=True)) + lam_init)
    v = v_ref[...]

    def one_map(t):
        q = q_ref[:, t * dh:(t + 1) * dh]
        k = k_ref[:, t * dh:(t + 1) * dh]
        s = lax.dot_general(q, k, (((1,), (1,)), ((), ())), preferred_element_type=F32)
        m = jnp.max(s, axis=-1, keepdims=True)
        p = jnp.exp(s - m)
        l = jnp.sum(p, axis=-1, keepdims=True)
        o = jnp.dot(p.astype(BF16), v, preferred_element_type=F32)
        return o / l

    o = one_map(0) - lam * one_map(1)
    r = lax.rsqrt(jnp.mean(o * o, axis=-1, keepdims=True) + SUBLN_EPS)
    o_ref[...] = (((o * r) * g_ref[...]) * (1.0 - lam_init)).astype(o_ref.dtype)


def _attention(qk, v, lq1, lk1, lq2, lk2, subln_g, b, s, heads, dh, dv, lam_init, tq=256):
    t = b * s
    tq = _tile(s, tq)
    nq = s // tq
    assert 2 * dh == dv
    lam_specs = [pl.BlockSpec((1, dh), lambda bi, h, qi: (0, 0))] * 4
    return pl.pallas_call(
        functools.partial(_attn_kernel, dh=dh, lam_init=lam_init),
        grid=(b, heads, nq),
        in_specs=[
            pl.BlockSpec((tq, 2 * dh), lambda bi, h, qi: (bi * nq + qi, h)),
            pl.BlockSpec((s, 2 * dh), lambda bi, h, qi: (bi, heads + h)),
            pl.BlockSpec((s, dv), lambda bi, h, qi: (bi, h)),
            *lam_specs,
            pl.BlockSpec((1, dv), lambda bi, h, qi: (0, 0)),
        ],
        out_specs=pl.BlockSpec((tq, dv), lambda bi, h, qi: (bi * nq + qi, h)),
        out_shape=jax.ShapeDtypeStruct((t, heads * dv), BF16),
        compiler_params=_params("parallel", "parallel", "parallel"),
        name="diff_attention",
    )(qk, qk, v, lq1.reshape(1, dh), lk1.reshape(1, dh), lq2.reshape(1, dh), lk2.reshape(1, dh),
      subln_g.reshape(1, dv))


def _router_kernel(h_ref, g_ref, wrt_ref, o_ref):
    h = h_ref[...]
    r = lax.rsqrt(jnp.mean(h * h, axis=-1, keepdims=True) + RMS_EPS)
    hn = (h * r) * g_ref[...]
    logits = lax.dot_general(wrt_ref[...], hn, (((1,), (1,)), ((), ())),
                             precision=lax.Precision.HIGHEST, preferred_element_type=F32)
    m = jnp.max(logits, axis=0, keepdims=True)
    p = jnp.exp(logits - m)
    o_ref[...] = p / jnp.sum(p, axis=0, keepdims=True)


def _router(h, g, w_router_t, tm=512):
    t, d = h.shape
    e = w_router_t.shape[0]
    tm = _tile(t, tm)
    return pl.pallas_call(
        _router_kernel,
        grid=(t // tm,),
        in_specs=[
            pl.BlockSpec((tm, d), lambda i: (i, 0)),
            pl.BlockSpec((1, d), lambda i: (0, 0)),
            pl.BlockSpec((e, d), lambda i: (0, 0)),
        ],
        out_specs=pl.BlockSpec((e, tm), lambda i: (0, i)),
        out_shape=jax.ShapeDtypeStruct((e, t), F32),
        compiler_params=_params("parallel"),
        name="router",
    )(h, g.reshape(1, d), w_router_t)


def _prefix_count(mask, tri):
    e, s = mask.shape
    carry = jnp.zeros((e, 1), F32)
    outs = []
    for c in range(s // V7X_LANES):
        chunk = mask[:, c * V7X_LANES:(c + 1) * V7X_LANES].astype(BF16)
        loc = jnp.dot(chunk, tri, preferred_element_type=F32) + carry
        outs.append(loc)
        carry = loc[:, V7X_LANES - 1:V7X_LANES]
    return jnp.concatenate(outs, axis=1)


def _topk_kernel(aff_ref, idx_ref, gate_ref, sel_ref, pref_ref, pos_ref, aff3_ref, *, cap, pc):
    b = pl.program_id(0)
    aff = aff_ref[...]
    e, s = aff.shape
    keys = pltpu.bitcast(aff, jnp.int32)

    def search(i, t):
        cand = t | jnp.left_shift(jnp.int32(1), 30 - i)
        cnt = jnp.sum(jnp.where(keys >= cand, 1.0, 0.0), axis=1, keepdims=True)
        return jnp.where(cnt >= cap, cand, t)

    thr = lax.fori_loop(0, 31, search, jnp.zeros((e, 1), jnp.int32))
    gt = keys > thr
    eq = keys == thr
    need = cap - jnp.sum(jnp.where(gt, 1.0, 0.0), axis=1, keepdims=True)
    ri = lax.broadcasted_iota(jnp.int32, (V7X_LANES, V7X_LANES), 0)
    ci = lax.broadcasted_iota(jnp.int32, (V7X_LANES, V7X_LANES), 1)
    tri = jnp.where(ri <= ci, 1.0, 0.0).astype(BF16)
    eq_rank = _prefix_count(jnp.where(eq, 1.0, 0.0), tri)
    sel = jnp.where(gt | (eq & (eq_rank <= need)), 1.0, 0.0)
    pref = _prefix_count(sel, tri)
    sel_ref[0] = sel
    pref_ref[0] = pref.astype(jnp.int32)
    pos = jnp.where(sel > 0.0, pref - 1.0, -1.0)
    for ei in range(e):
        pos_ref[ei] = pos[ei:ei + 1, :]
        aff3_ref[ei] = aff[ei:ei + 1, :]

    tok = lax.broadcasted_iota(jnp.int32, (1, s), 1)
    tok_hi = (tok // 64).astype(F32)
    tok_lo = (tok % 64).astype(F32)
    row = lax.broadcasted_iota(jnp.int32, (2 * V7X_SUBLANES, 1), 0)

    def invert(ei, carry):
        a = aff3_ref[ei]
        hi = a.astype(BF16).astype(F32)
        r1 = a - hi
        mid = r1.astype(BF16).astype(F32)
        lo = r1 - mid
        lhs = jnp.where(row == 0, tok_hi, jnp.where(row == 1, tok_lo, jnp.where(
            row == 2, hi, jnp.where(row == 3, mid, jnp.where(row == 4, lo, 0.0))))).astype(BF16)
        posrow = pos_ref[ei]
        for c in range(cap // pc):
            slot = (lax.broadcasted_iota(jnp.int32, (pc, 1), 0) + c * pc).astype(F32)
            onehot = jnp.where(slot == posrow, 1.0, 0.0).astype(BF16)
            res = lax.dot_general(lhs, onehot, (((1,), (1,)), ((), ())), preferred_element_type=F32)
            idx = res[0:1] * 64.0 + res[1:2]
            idx_ref[0, ei, :, c * pc:(c + 1) * pc] = idx.astype(jnp.int32) + b * s
            gate_ref[0, ei, :, c * pc:(c + 1) * pc] = (res[2:3] + res[3:4]) + res[4:5]
        return carry

    lax.fori_loop(0, e, invert, 0)


def _topk(aff_t, b, s, cap):
    e = aff_t.shape[0]
    pc = _tile(cap, 128)
    return pl.pallas_call(
        functools.partial(_topk_kernel, cap=cap, pc=pc),
        grid=(b,),
        in_specs=[pl.BlockSpec((e, s), lambda bi: (0, bi))],
        out_specs=[
            pl.BlockSpec((1, e, 1, cap), lambda bi: (bi, 0, 0, 0)),
            pl.BlockSpec((1, e, 1, cap), lambda bi: (bi, 0, 0, 0)),
            pl.BlockSpec((1, e, s), lambda bi: (bi, 0, 0)),
            pl.BlockSpec((1, e, s), lambda bi: (bi, 0, 0)),
        ],
        out_shape=[
            jax.ShapeDtypeStruct((b, e, 1, cap), jnp.int32),
            jax.ShapeDtypeStruct((b, e, 1, cap), F32),
            jax.ShapeDtypeStruct((b, e, s), F32),
            jax.ShapeDtypeStruct((b, e, s), jnp.int32),
        ],
        scratch_shapes=[pltpu.VMEM((e, 1, s), F32), pltpu.VMEM((e, 1, s), F32)],
        compiler_params=_params("parallel"),
        name="expert_choice_topk",
    )(aff_t)


def _gather_kernel(idx_ref, h_ref, g_ref, o_ref, buf_ref, sem_ref, *, cap):
    step = pl.program_id(0)
    nsteps = pl.num_programs(0)

    def issue(s, slot):
        def body(p, carry):
            row = idx_ref[s * cap + p]
            pltpu.make_async_copy(h_ref.at[pl.ds(row, 1)], buf_ref.at[slot, pl.ds(p, 1)],
                                  sem_ref.at[slot]).start()
            return carry
        lax.fori_loop(0, cap, body, 0)

    @pl.when(step == 0)
    def _():
        issue(0, 0)

    @pl.when(step + 1 < nsteps)
    def _():
        issue(step + 1, (step + 1) % 2)

    slot = step % 2
    pltpu.make_async_copy(h_ref.at[pl.ds(0, cap)], buf_ref.at[slot], sem_ref.at[slot]).wait()
    x = buf_ref[slot]
    r = lax.rsqrt(jnp.mean(x * x, axis=-1, keepdims=True) + RMS_EPS)
    o_ref[0] = ((x * r) * g_ref[...]).astype(o_ref.dtype)


def _gather_norm(h, idx_flat, g, e, b, cap):
    t, d = h.shape
    return pl.pallas_call(
        functools.partial(_gather_kernel, cap=cap),
        grid_spec=pltpu.PrefetchScalarGridSpec(
            num_scalar_prefetch=1,
            grid=(e * b,),
            in_specs=[
                pl.BlockSpec(memory_space=pl.ANY),
                pl.BlockSpec((1, d), lambda s, idx: (0, 0)),
            ],
            out_specs=pl.BlockSpec((1, cap, d), lambda s, idx: (s // b, s % b, 0)),
            scratch_shapes=[pltpu.VMEM((2, cap, d), F32), pltpu.SemaphoreType.DMA((2,))],
        ),
        out_shape=jax.ShapeDtypeStruct((e, b * cap, d), BF16),
        compiler_params=_params("arbitrary"),
        name="gather_rmsnorm",
    )(idx_flat, h, g.reshape(1, d))


def _combine_kernel(idx_ref, tab_ref, h_ref, sel_ref, y_ref, g_ref, o_ref, z_ref, sem_ref,
                    *, n_exp, nb, seq, cap, tt, final_norm):
    step = pl.program_id(0)
    nsteps = pl.num_programs(0)
    nt = seq // tt

    def bounds(s, ei):
        bi = s // nt
        base = (bi * n_exp + ei) * (nt + 1) + s % nt
        return tab_ref[base], tab_ref[base + 1]

    def issue(s, slot):
        bi = s // nt
        tok0 = bi * seq + (s % nt) * tt
        for ei in range(n_exp):
            lo, hi = bounds(s, ei)
            src0 = (ei * nb + bi) * cap

            def body(p, carry):
                dst = ei * tt + idx_ref[src0 + p] - tok0
                pltpu.make_async_copy(y_ref.at[pl.ds(src0 + p, 1)], z_ref.at[slot, pl.ds(dst, 1)],
                                      sem_ref.at[slot]).start()
                return carry
            lax.fori_loop(lo, hi, body, 0)

    def count(s):
        n = jnp.int32(0)
        for ei in range(n_exp):
            lo, hi = bounds(s, ei)
            n = n + (hi - lo)
        return n

    @pl.when(step == 0)
    def _():
        z_ref[...] = jnp.zeros(z_ref.shape, z_ref.dtype)
        issue(0, 0)

    @pl.when(step + 1 < nsteps)
    def _():
        issue(step + 1, (step + 1) % 2)

    slot = step % 2
    n = count(step)

    def wait_row(p, carry):
        pltpu.make_async_copy(y_ref.at[pl.ds(0, 1)], z_ref.at[slot, pl.ds(0, 1)], sem_ref.at[slot]).wait()
        return carry

    lax.fori_loop(0, n, wait_row, 0)

    acc = h_ref[...]
    sel = sel_ref[...]
    for ei in range(n_exp):
        m = sel[:, ei:ei + 1] > 0.0
        acc = acc + jnp.where(m, z_ref[slot, ei * tt:(ei + 1) * tt, :], 0.0)
    if final_norm:
        r = lax.rsqrt(jnp.mean(acc * acc, axis=-1, keepdims=True) + RMS_EPS)
        acc = (acc * r) * g_ref[...]
    o_ref[...] = acc


def _combine(h, y, idx_flat, tab_flat, sel_t, g, e, b, s, cap, final_norm, tt=64):
    t, d = h.shape
    tt = _tile(s, tt)
    return pl.pallas_call(
        functools.partial(_combine_kernel, n_exp=e, nb=b, seq=s, cap=cap, tt=tt, final_norm=final_norm),
        grid_spec=pltpu.PrefetchScalarGridSpec(
            num_scalar_prefetch=2,
            grid=(t // tt,),
            in_specs=[
                pl.BlockSpec((tt, d), lambda i, idx, tab: (i, 0)),
                pl.BlockSpec((tt, e), lambda i, idx, tab: (i, 0)),
                pl.BlockSpec(memory_space=pl.ANY),
                pl.BlockSpec((1, d), lambda i, idx, tab: (0, 0)),
            ],
            out_specs=pl.BlockSpec((tt, d), lambda i, idx, tab: (i, 0)),
            scratch_shapes=[pltpu.VMEM((2, e * tt, d), F32), pltpu.SemaphoreType.DMA((2,))],
        ),
        out_shape=jax.ShapeDtypeStruct((t, d), F32),
        compiler_params=_params("arbitrary"),
        name="combine",
    )(idx_flat, tab_flat, h, sel_t, y, g.reshape(1, d))


def kernel(x, positions, norm1_g, w_in, conv_w, conv_b, conv_ln_g, conv_ln_b, lq1, lk1, lq2, lk2,
           subln_g, w_out, norm2_g, w_router, w_gate, w_up, w_down, final_g):
    b, s, d = x.shape
    t = b * s
    depth = norm1_g.shape[0]
    c_conv = conv_w.shape[2]
    dh = lq1.shape[1]
    dv = subln_g.shape[1]
    n_exp = w_router.shape[2]
    in_cols = w_in.shape[2]
    v_cols = w_out.shape[1] - c_conv
    heads = v_cols // dv
    qk_cols = heads * 2 * dh
    assert in_cols == 2 * c_conv + 2 * qk_cols + v_cols
    cap = CAPACITY_FACTOR * s // n_exp
    c1 = 2 * c_conv
    c3 = c1 + 2 * qk_cols

    inv_freq = ROPE_THETA ** (-jnp.arange(0, dh, 2, dtype=F32) / dh)
    ang = positions.astype(F32).reshape(t, 1) * inv_freq
    cos_t = jnp.concatenate([jnp.cos(ang), jnp.cos(ang)], axis=-1)
    sin_t = jnp.concatenate([-jnp.sin(ang), jnp.sin(ang)], axis=-1)

    xf = x.reshape(t, d)
    tt = _tile(s, 64)
    nt = s // tt
    for l in range(depth):
        lam_init = 0.8 - 0.6 * math.exp(-0.3 * l)
        w_in_b = w_in[l].astype(BF16)
        w_out_b = w_out[l].astype(BF16)

        xn = _rmsnorm(xf, norm1_g[l], RMS_EPS, BF16)
        glu = _glu_mm(xn[None], w_in_b[None], w_in_b[None], c_conv, 0, c_conv, False, F32)[0]
        qk = _rope_mm(xn, w_in_b, c1, 2 * qk_cols, qk_cols, cos_t, sin_t, dh ** -0.5, dh)
        v = _mm(xn, w_in_b, c3, v_cols, BF16)

        conv_out = _conv(glu.reshape(b, s, c_conv), conv_w[l], conv_b[l], conv_ln_g[l], conv_ln_b[l])
        attn = _attention(qk, v, lq1[l], lk1[l], lq2[l], lk2[l], subln_g[l], b, s, heads, dh, dv, lam_init)
        h = _out_mm(conv_out.reshape(t, c_conv), attn, w_out_b, xf)

        aff_t = _router(h, norm2_g[l], w_router[l].T)
        idx, gates, sel, pref = _topk(aff_t, b, s, cap)
        idx_flat = idx.reshape(b, n_exp, cap).transpose(1, 0, 2).reshape(-1)
        gates_e = gates.reshape(b, n_exp, cap).transpose(1, 0, 2).reshape(n_exp, b * cap, 1)
        sel_t = sel.transpose(0, 2, 1).reshape(t, n_exp)
        tab = jnp.concatenate([jnp.zeros((b, n_exp, 1), jnp.int32), pref[:, :, tt - 1::tt]], axis=-1)

        xs = _gather_norm(h, idx_flat, norm2_g[l], n_exp, b, cap)
        act = _glu_mm(xs, w_gate[l].astype(BF16), w_up[l].astype(BF16), w_gate.shape[3], 0, 0, True, BF16)
        y = _down_mm(act, w_down[l].astype(BF16), gates_e, F32)
        xf = _combine(h, y.reshape(n_exp * b * cap, d), idx_flat, tab.reshape(-1), sel_t, final_g,
                      n_exp, b, s, cap, final_norm=(l == depth - 1), tt=tt)
    return xf.reshape(b, s, d)
```

```python
import functools
import math

import jax
import jax.numpy as jnp
from jax import lax
from jax.experimental import pallas as pl
from jax.experimental.pallas import tpu as pltpu

F32 = jnp.float32
BF16 = jnp.bfloat16

V7X_LANES = 128
V7X_SUBLANES = 8
V7X_VMEM_LIMIT_BYTES = 56 * 1024 * 1024

ROPE_THETA = 10000.0
RMS_EPS = 1e-6
LN_EPS = 1e-5
SUBLN_EPS = 1e-5
CAPACITY_FACTOR = 2
LOG2_E = math.log2(math.e)
CONV_HALO = 16


def _params(*sem):
    return pltpu.CompilerParams(dimension_semantics=sem, vmem_limit_bytes=V7X_VMEM_LIMIT_BYTES)


def _tile(n, want):
    t = min(n, want)
    assert n % t == 0, (n, want)
    return t


def _rmsnorm_kernel(x_ref, g_ref, o_ref, *, eps):
    x = x_ref[...]
    r = lax.rsqrt(jnp.mean(x * x, axis=-1, keepdims=True) + eps)
    o_ref[...] = ((x * r) * g_ref[...]).astype(o_ref.dtype)


def _rmsnorm(x, g, eps, out_dtype):
    t, d = x.shape
    tm = _tile(t, 256)
    return pl.pallas_call(
        functools.partial(_rmsnorm_kernel, eps=eps),
        grid=(t // tm,),
        in_specs=[pl.BlockSpec((tm, d), lambda i: (i, 0)), pl.BlockSpec((1, d), lambda i: (0, 0))],
        out_specs=pl.BlockSpec((tm, d), lambda i: (i, 0)),
        out_shape=jax.ShapeDtypeStruct((t, d), out_dtype),
        compiler_params=_params("parallel"),
        name="rmsnorm",
    )(x, g.reshape(1, d))


def _cast_resident(w_refs, w_scratch, row_axis):
    @pl.when(pl.program_id(row_axis) == 0)
    def _():
        for w_ref, w_s in zip(w_refs, w_scratch):
            w_s[...] = w_ref[...].astype(BF16)


def _glu_act(ua, ub, swiglu):
    if swiglu:
        return (ua * jax.nn.sigmoid(ua)) * ub
    return ua * jax.nn.sigmoid(ub)


def _glu_mm_kernel(a_ref, wa_ref, wb_ref, o_ref, wa_s, wb_s, *, swiglu):
    _cast_resident((wa_ref, wb_ref), (wa_s, wb_s), 1)
    a = a_ref[...]
    ua = jnp.dot(a, wa_s[...], preferred_element_type=F32)
    ub = jnp.dot(a, wb_s[...], preferred_element_type=F32)
    o_ref[...] = _glu_act(ua, ub, swiglu).astype(o_ref.dtype)


def _glu_mm(a, w, n, a_col, b_col, swiglu, out_dtype, tm=1024, tn=256):
    m, k = a.shape
    tm = _tile(m, tm)
    tn = _tile(n, tn)
    assert a_col % tn == 0 and b_col % tn == 0
    ao, bo = a_col // tn, b_col // tn
    return pl.pallas_call(
        functools.partial(_glu_mm_kernel, swiglu=swiglu),
        grid=(n // tn, m // tm),
        in_specs=[
            pl.BlockSpec((tm, k), lambda j, i: (i, 0)),
            pl.BlockSpec((k, tn), lambda j, i: (0, j + ao)),
            pl.BlockSpec((k, tn), lambda j, i: (0, j + bo)),
        ],
        out_specs=pl.BlockSpec((tm, tn), lambda j, i: (i, j)),
        out_shape=jax.ShapeDtypeStruct((m, n), out_dtype),
        scratch_shapes=[pltpu.VMEM((k, tn), BF16), pltpu.VMEM((k, tn), BF16)],
        compiler_params=_params("parallel", "arbitrary"),
        name="glu_matmul",
    )(a, w, w)


def _expert_up_kernel(a_ref, wg_ref, wu_ref, o_ref):
    a = a_ref[0]
    hg = jnp.dot(a, wg_ref[0].astype(BF16), preferred_element_type=F32)
    hu = jnp.dot(a, wu_ref[0].astype(BF16), preferred_element_type=F32)
    o_ref[0] = _glu_act(hg, hu, True).astype(o_ref.dtype)


def _expert_up(a, wg, wu, tn=256):
    g, m, k = a.shape
    n = wg.shape[2]
    tn = _tile(n, tn)
    return pl.pallas_call(
        _expert_up_kernel,
        grid=(g, n // tn),
        in_specs=[
            pl.BlockSpec((1, m, k), lambda e, j: (e, 0, 0), pipeline_mode=pl.Buffered(1)),
            pl.BlockSpec((1, k, tn), lambda e, j: (e, 0, j)),
            pl.BlockSpec((1, k, tn), lambda e, j: (e, 0, j)),
        ],
        out_specs=pl.BlockSpec((1, m, tn), lambda e, j: (e, 0, j)),
        out_shape=jax.ShapeDtypeStruct((g, m, n), BF16),
        compiler_params=_params("parallel", "parallel"),
        name="expert_up",
    )(a, wg, wu)


def _rope_mm_kernel(a_ref, w_ref, cos_ref, sin_ref, o_ref, w_s, *, n_q_tiles, q_scale, dh):
    _cast_resident((w_ref,), (w_s,), 1)
    u = jnp.dot(a_ref[...], w_s[...], preferred_element_type=F32)
    cos = cos_ref[...]
    sin = sin_ref[...]
    scale = jnp.where(pl.program_id(0) < n_q_tiles, q_scale, 1.0).astype(F32)
    tn = u.shape[1]
    for c in range(tn // dh):
        t = u[:, c * dh:(c + 1) * dh]
        r = t * cos + pltpu.roll(t, dh // 2, axis=1) * sin
        o_ref[:, c * dh:(c + 1) * dh] = (r * scale).astype(o_ref.dtype)


def _rope_mm(a, w, col, n, n_q, cos, sin, q_scale, dh, tm=1024, tn=512):
    m, k = a.shape
    tm = _tile(m, tm)
    tn = _tile(n_q, tn)
    assert col % tn == 0 and n % tn == 0 and tn % dh == 0
    co = col // tn
    return pl.pallas_call(
        functools.partial(_rope_mm_kernel, n_q_tiles=n_q // tn, q_scale=q_scale, dh=dh),
        grid=(n // tn, m // tm),
        in_specs=[
            pl.BlockSpec((tm, k), lambda j, i: (i, 0)),
            pl.BlockSpec((k, tn), lambda j, i: (0, j + co)),
            pl.BlockSpec((tm, dh), lambda j, i: (i, 0)),
            pl.BlockSpec((tm, dh), lambda j, i: (i, 0)),
        ],
        out_specs=pl.BlockSpec((tm, tn), lambda j, i: (i, j)),
        out_shape=jax.ShapeDtypeStruct((m, n), BF16),
        scratch_shapes=[pltpu.VMEM((k, tn), BF16)],
        compiler_params=_params("parallel", "arbitrary"),
        name="rope_matmul",
    )(a, w, cos, sin)


def _mm_kernel(a_ref, w_ref, o_ref, w_s):
    _cast_resident((w_ref,), (w_s,), 1)
    o_ref[...] = jnp.dot(a_ref[...], w_s[...], preferred_element_type=F32).astype(o_ref.dtype)


def _mm(a, w, col, n, out_dtype, tm=1024, tn=512):
    m, k = a.shape
    tm = _tile(m, tm)
    tn = _tile(n, tn)
    assert col % tn == 0
    co = col // tn
    return pl.pallas_call(
        _mm_kernel,
        grid=(n // tn, m // tm),
        in_specs=[pl.BlockSpec((tm, k), lambda j, i: (i, 0)), pl.BlockSpec((k, tn), lambda j, i: (0, j + co))],
        out_specs=pl.BlockSpec((tm, tn), lambda j, i: (i, j)),
        out_shape=jax.ShapeDtypeStruct((m, n), out_dtype),
        scratch_shapes=[pltpu.VMEM((k, tn), BF16)],
        compiler_params=_params("parallel", "arbitrary"),
        name="matmul",
    )(a, w)


def _out_mm_kernel(a1_ref, a2_ref, w1_ref, w2_ref, x_ref, o_ref, w1_s, w2_s):
    _cast_resident((w1_ref, w2_ref), (w1_s, w2_s), 1)
    acc = jnp.dot(a1_ref[...], w1_s[...], preferred_element_type=F32)
    acc = acc + jnp.dot(a2_ref[...], w2_s[...], preferred_element_type=F32)
    o_ref[...] = x_ref[...] + acc


def _out_mm(a1, a2, w, x, tm=1024, tn=512):
    m, k1 = a1.shape
    k2 = a2.shape[1]
    n = w.shape[1]
    assert k1 == k2 and w.shape[0] == k1 + k2
    tm = _tile(m, tm)
    tn = _tile(n, tn)
    return pl.pallas_call(
        _out_mm_kernel,
        grid=(n // tn, m // tm),
        in_specs=[
            pl.BlockSpec((tm, k1), lambda j, i: (i, 0)),
            pl.BlockSpec((tm, k2), lambda j, i: (i, 0)),
            pl.BlockSpec((k1, tn), lambda j, i: (0, j)),
            pl.BlockSpec((k2, tn), lambda j, i: (1, j)),
            pl.BlockSpec((tm, tn), lambda j, i: (i, j)),
        ],
        out_specs=pl.BlockSpec((tm, tn), lambda j, i: (i, j)),
        out_shape=jax.ShapeDtypeStruct((m, n), F32),
        scratch_shapes=[pltpu.VMEM((k1, tn), BF16), pltpu.VMEM((k2, tn), BF16)],
        compiler_params=_params("parallel", "arbitrary"),
        name="out_matmul",
    )(a1, a2, w, w, x)


def _down_mm_kernel(a_ref, w_ref, gate_ref, o_ref):
    acc = jnp.dot(a_ref[0], w_ref[0].astype(BF16), preferred_element_type=F32)
    o_ref[0] = (acc * gate_ref[0]).astype(o_ref.dtype)


def _down_mm(a, w, gates, out_dtype, tn=256):
    g, m, k = a.shape
    n = w.shape[2]
    tn = _tile(n, tn)
    return pl.pallas_call(
        _down_mm_kernel,
        grid=(g, n // tn),
        in_specs=[
            pl.BlockSpec((1, m, k), lambda e, j: (e, 0, 0), pipeline_mode=pl.Buffered(1)),
            pl.BlockSpec((1, k, tn), lambda e, j: (e, 0, j)),
            pl.BlockSpec((1, m, 1), lambda e, j: (e, 0, 0)),
        ],
        out_specs=pl.BlockSpec((1, m, tn), lambda e, j: (e, 0, j)),
        out_shape=jax.ShapeDtypeStruct((g, m, n), out_dtype),
        compiler_params=_params("parallel", "parallel"),
        name="expert_down",
    )(a, w, gates)


def _conv_kernel(prev_ref, cur_ref, next_ref, cw_ref, cb_ref, lg_ref, lb_ref, o_ref,
                 win_ref, sh_ref, y_ref, *, ts, width, rc):
    i = pl.program_id(1)
    n = pl.num_programs(1)
    halo = CONV_HALO
    pad = width // 2
    c = cur_ref.shape[2]
    win_ref[0:halo, :] = jnp.where(i > 0, prev_ref[0], 0.0)
    win_ref[halo:halo + ts, :] = cur_ref[0]
    win_ref[halo + ts:halo + ts + halo, :] = jnp.where(i < n - 1, next_ref[0], 0.0)
    span = ts + 3 * V7X_SUBLANES
    for r in range(V7X_SUBLANES):
        sh_ref[r] = win_ref[r:r + span, :]

    def chunk_body(rb, carry):
        r0 = pl.multiple_of(rb * rc, rc)
        for cc in range(c // V7X_LANES):
            lanes = slice(cc * V7X_LANES, (cc + 1) * V7X_LANES)
            acc = jnp.zeros((rc, V7X_LANES), F32) + cb_ref[:, lanes]
            for w in range(width):
                off = halo - pad + w
                rows = pl.ds(pl.multiple_of(r0 + V7X_SUBLANES * (off // V7X_SUBLANES), V7X_SUBLANES), rc)
                acc = acc + sh_ref[off % V7X_SUBLANES, rows, lanes] * cw_ref[w:w + 1, lanes]
            y_ref[pl.ds(r0, rc), lanes] = acc
        return carry

    lax.fori_loop(0, ts // rc, chunk_body, 0)
    y = y_ref[...]
    mu = jnp.mean(y, axis=-1, keepdims=True)
    yc = y - mu
    var = jnp.mean(yc * yc, axis=-1, keepdims=True)
    z = (yc * lax.rsqrt(var + LN_EPS)) * lg_ref[...] + lb_ref[...]
    o_ref[0] = (z * jax.nn.sigmoid(z)).astype(o_ref.dtype)


def _conv(glu, conv_w, conv_b, ln_g, ln_b, ts=256, rc=64):
    b, s, c = glu.shape
    width = conv_w.shape[0]
    halo = CONV_HALO
    assert width // 2 < halo and c % V7X_LANES == 0
    ts = _tile(s, ts)
    rc = _tile(ts, rc)
    nh = ts // halo
    last = s // halo - 1
    return pl.pallas_call(
        functools.partial(_conv_kernel, ts=ts, width=width, rc=rc),
        grid=(b, s // ts),
        in_specs=[
            pl.BlockSpec((1, halo, c), lambda bi, i: (bi, jnp.maximum(i * nh - 1, 0), 0)),
            pl.BlockSpec((1, ts, c), lambda bi, i: (bi, i, 0)),
            pl.BlockSpec((1, halo, c), lambda bi, i: (bi, jnp.minimum((i + 1) * nh, last), 0)),
            pl.BlockSpec((width, c), lambda bi, i: (0, 0)),
            pl.BlockSpec((1, c), lambda bi, i: (0, 0)),
            pl.BlockSpec((1, c), lambda bi, i: (0, 0)),
            pl.BlockSpec((1, c), lambda bi, i: (0, 0)),
        ],
        out_specs=pl.BlockSpec((1, ts, c), lambda bi, i: (bi, i, 0)),
        out_shape=jax.ShapeDtypeStruct((b, s, c), BF16),
        scratch_shapes=[
            pltpu.VMEM((ts + 2 * halo, c), F32),
            pltpu.VMEM((V7X_SUBLANES, ts + 3 * V7X_SUBLANES, c), F32),
            pltpu.VMEM((ts, c), F32),
        ],
        compiler_params=_params("parallel", "parallel"),
        name="conformer_conv",
    )(glu, glu, glu, conv_w, conv_b.reshape(1, c), ln_g.reshape(1, c), ln_b.reshape(1, c))


def _attn_kernel(q_ref, k_ref, v_ref, lq1_ref, lk1_ref, lq2_ref, lk2_ref, g_ref, o_ref, *, dh, lam_init, kc):
    lam = (jnp.exp(jnp.sum(lq1_ref[...] * lk1_ref[...], keepdims=True))
           - jnp.exp(jnp.sum(lq2_ref[...] * lk2_ref[...], keepdims=True)) + lam_init)
    n_keys = k_ref.shape[0]

    def one_map(t):
        q = q_ref[:, t * dh:(t + 1) * dh]
        parts = []
        nck = n_keys // kc
        for c in range(nck):
            k = k_ref[c * kc:(c + 1) * kc, t * dh:(t + 1) * dh]
            s = lax.dot_general(q, k, (((1,), (1,)), ((), ())), preferred_element_type=F32)
            m = jnp.max(s, axis=-1, keepdims=True)
            p = jnp.exp2(s - m)
            l = jnp.sum(p, axis=-1, keepdims=True)
            o = jnp.dot(p.astype(BF16), v_ref[c * kc:(c + 1) * kc, :], preferred_element_type=F32)
            parts.append((m, l, o))
        m_all = functools.reduce(jnp.maximum, [m for m, _, _ in parts])
        l_all = 0.0
        o_all = 0.0
        for m, l, o in parts:
            w = jnp.exp2(m - m_all)
            l_all = l_all + l * w
            o_all = o_all + o * w
        return o_all / l_all

    o = one_map(0) - lam * one_map(1)
    r = lax.rsqrt(jnp.mean(o * o, axis=-1, keepdims=True) + SUBLN_EPS)
    o_ref[...] = (((o * r) * g_ref[...]) * (1.0 - lam_init)).astype(o_ref.dtype)


def _attention(qk, v, lq1, lk1, lq2, lk2, subln_g, b, s, heads, dh, dv, lam_init, tq=512, kc=1024):
    t = b * s
    tq = _tile(s, tq)
    kc = _tile(s, kc)
    nq = s // tq
    assert 2 * dh == dv
    lam_specs = [pl.BlockSpec((1, dh), lambda bi, h, qi: (0, 0))] * 4
    return pl.pallas_call(
        functools.partial(_attn_kernel, dh=dh, lam_init=lam_init, kc=kc),
        grid=(b, heads, nq),
        in_specs=[
            pl.BlockSpec((tq, 2 * dh), lambda bi, h, qi: (bi * nq + qi, h)),
            pl.BlockSpec((s, 2 * dh), lambda bi, h, qi: (bi, heads + h)),
            pl.BlockSpec((s, dv), lambda bi, h, qi: (bi, h)),
            *lam_specs,
            pl.BlockSpec((1, dv), lambda bi, h, qi: (0, 0)),
        ],
        out_specs=pl.BlockSpec((tq, dv), lambda bi, h, qi: (bi * nq + qi, h)),
        out_shape=jax.ShapeDtypeStruct((t, heads * dv), BF16),
        compiler_params=_params("parallel", "parallel", "parallel"),
        name="diff_attention",
    )(qk, qk, v, lq1.reshape(1, dh), lk1.reshape(1, dh), lq2.reshape(1, dh), lk2.reshape(1, dh),
      subln_g.reshape(1, dv))


def _router_kernel(h_ref, g_ref, wrt_ref, o_ref):
    h = h_ref[...]
    r = lax.rsqrt(jnp.mean(h * h, axis=-1, keepdims=True) + RMS_EPS)
    hn = (h * r) * g_ref[...]
    logits = lax.dot_general(wrt_ref[...], hn, (((1,), (1,)), ((), ())),
                             precision=lax.Precision.HIGHEST, preferred_element_type=F32)
    m = jnp.max(logits, axis=0, keepdims=True)
    p = jnp.exp(logits - m)
    o_ref[...] = p / jnp.sum(p, axis=0, keepdims=True)


def _router(h, g, w_router_t, tm=512):
    t, d = h.shape
    e = w_router_t.shape[0]
    tm = _tile(t, tm)
    return pl.pallas_call(
        _router_kernel,
        grid=(t // tm,),
        in_specs=[
            pl.BlockSpec((tm, d), lambda i: (i, 0)),
            pl.BlockSpec((1, d), lambda i: (0, 0)),
            pl.BlockSpec((e, d), lambda i: (0, 0)),
        ],
        out_specs=pl.BlockSpec((e, tm), lambda i: (0, i)),
        out_shape=jax.ShapeDtypeStruct((e, t), F32),
        compiler_params=_params("parallel"),
        name="router",
    )(h, g.reshape(1, d), w_router_t)


def _prefix_count(mask, tri):
    e, s = mask.shape
    carry = jnp.zeros((e, 1), F32)
    outs = []
    for c in range(s // V7X_LANES):
        chunk = mask[:, c * V7X_LANES:(c + 1) * V7X_LANES].astype(BF16)
        loc = jnp.dot(chunk, tri, preferred_element_type=F32) + carry
        outs.append(loc)
        carry = loc[:, V7X_LANES - 1:V7X_LANES]
    return jnp.concatenate(outs, axis=1)


def _topk_kernel(aff_ref, idx_ref, gate_ref, sel_ref, pref_ref, pos_ref, aff3_ref, *, cap, pc):
    b = pl.program_id(0)
    aff = aff_ref[...]
    e, s = aff.shape
    keys = pltpu.bitcast(aff, jnp.int32)

    def search(i, t):
        cand = t | jnp.left_shift(jnp.int32(1), 30 - i)
        cnt = jnp.sum(jnp.where(keys >= cand, 1.0, 0.0), axis=1, keepdims=True)
        return jnp.where(cnt >= cap, cand, t)

    thr = lax.fori_loop(0, 31, search, jnp.zeros((e, 1), jnp.int32))
    gt = keys > thr
    eq = keys == thr
    need = cap - jnp.sum(jnp.where(gt, 1.0, 0.0), axis=1, keepdims=True)
    ri = lax.broadcasted_iota(jnp.int32, (V7X_LANES, V7X_LANES), 0)
    ci = lax.broadcasted_iota(jnp.int32, (V7X_LANES, V7X_LANES), 1)
    tri = jnp.where(ri <= ci, 1.0, 0.0).astype(BF16)
    eq_rank = _prefix_count(jnp.where(eq, 1.0, 0.0), tri)
    sel = jnp.where(gt | (eq & (eq_rank <= need)), 1.0, 0.0)
    pref = _prefix_count(sel, tri)
    sel_ref[0] = sel
    pref_ref[0] = pref.astype(jnp.int32)
    pos = jnp.where(sel > 0.0, pref - 1.0, -1.0)
    for ei in range(e):
        pos_ref[ei] = pos[ei:ei + 1, :]
        aff3_ref[ei] = aff[ei:ei + 1, :]

    tok = lax.broadcasted_iota(jnp.int32, (1, s), 1)
    tok_hi = (tok // 64).astype(F32)
    tok_lo = (tok % 64).astype(F32)
    row = lax.broadcasted_iota(jnp.int32, (2 * V7X_SUBLANES, 1), 0)

    def invert(ei, carry):
        a = aff3_ref[ei]
        hi = a.astype(BF16).astype(F32)
        r1 = a - hi
        mid = r1.astype(BF16).astype(F32)
        lo = r1 - mid
        lhs = jnp.where(row == 0, tok_hi, jnp.where(row == 1, tok_lo, jnp.where(
            row == 2, hi, jnp.where(row == 3, mid, jnp.where(row == 4, lo, 0.0))))).astype(BF16)
        posrow = pos_ref[ei]
        for c in range(cap // pc):
            slot = (lax.broadcasted_iota(jnp.int32, (pc, 1), 0) + c * pc).astype(F32)
            onehot = jnp.where(slot == posrow, 1.0, 0.0).astype(BF16)
            res = lax.dot_general(lhs, onehot, (((1,), (1,)), ((), ())), preferred_element_type=F32)
            idx = res[0:1] * 64.0 + res[1:2]
            idx_ref[0, ei, :, c * pc:(c + 1) * pc] = idx.astype(jnp.int32) + b * s
            gate_ref[0, ei, :, c * pc:(c + 1) * pc] = (res[2:3] + res[3:4]) + res[4:5]
        return carry

    lax.fori_loop(0, e, invert, 0)


def _topk(aff_t, b, s, cap):
    e = aff_t.shape[0]
    pc = _tile(cap, 128)
    return pl.pallas_call(
        functools.partial(_topk_kernel, cap=cap, pc=pc),
        grid=(b,),
        in_specs=[pl.BlockSpec((e, s), lambda bi: (0, bi))],
        out_specs=[
            pl.BlockSpec((1, e, 1, cap), lambda bi: (bi, 0, 0, 0)),
            pl.BlockSpec((1, e, 1, cap), lambda bi: (bi, 0, 0, 0)),
            pl.BlockSpec((1, e, s), lambda bi: (bi, 0, 0)),
            pl.BlockSpec((1, e, s), lambda bi: (bi, 0, 0)),
        ],
        out_shape=[
            jax.ShapeDtypeStruct((b, e, 1, cap), jnp.int32),
            jax.ShapeDtypeStruct((b, e, 1, cap), F32),
            jax.ShapeDtypeStruct((b, e, s), F32),
            jax.ShapeDtypeStruct((b, e, s), jnp.int32),
        ],
        scratch_shapes=[pltpu.VMEM((e, 1, s), F32), pltpu.VMEM((e, 1, s), F32)],
        compiler_params=_params("parallel"),
        name="expert_choice_topk",
    )(aff_t)


def _gather_kernel(idx_ref, h_ref, g_ref, o_ref, buf_ref, sem_ref, *, cap):
    step = pl.program_id(0)
    nsteps = pl.num_programs(0)

    def issue(s, slot):
        def body(p, carry):
            row = idx_ref[s * cap + p]
            pltpu.make_async_copy(h_ref.at[pl.ds(row, 1)], buf_ref.at[slot, pl.ds(p, 1)],
                                  sem_ref.at[slot]).start()
            return carry
        lax.fori_loop(0, cap, body, 0, unroll=8)

    @pl.when(step == 0)
    def _():
        issue(0, 0)

    @pl.when(step + 1 < nsteps)
    def _():
        issue(step + 1, (step + 1) % 2)

    slot = step % 2
    pltpu.make_async_copy(h_ref.at[pl.ds(0, cap)], buf_ref.at[slot], sem_ref.at[slot]).wait()
    x = buf_ref[slot]
    r = lax.rsqrt(jnp.mean(x * x, axis=-1, keepdims=True) + RMS_EPS)
    o_ref[0] = ((x * r) * g_ref[...]).astype(o_ref.dtype)


def _gather_norm(h, idx_flat, g, e, b, cap):
    t, d = h.shape
    return pl.pallas_call(
        functools.partial(_gather_kernel, cap=cap),
        grid_spec=pltpu.PrefetchScalarGridSpec(
            num_scalar_prefetch=1,
            grid=(e * b,),
            in_specs=[
                pl.BlockSpec(memory_space=pl.ANY),
                pl.BlockSpec((1, d), lambda s, idx: (0, 0)),
            ],
            out_specs=pl.BlockSpec((1, cap, d), lambda s, idx: (s // b, s % b, 0)),
            scratch_shapes=[pltpu.VMEM((2, cap, d), F32), pltpu.SemaphoreType.DMA((2,))],
        ),
        out_shape=jax.ShapeDtypeStruct((e, b * cap, d), BF16),
        compiler_params=_params("arbitrary"),
        name="gather_rmsnorm",
    )(idx_flat, h, g.reshape(1, d))


def _combine_kernel(idx_ref, tab_ref, h_ref, sel_ref, y_ref, g_ref, o_ref, z_ref, sem_ref,
                    *, n_exp, nb, seq, cap, tt, final_norm):
    step = pl.program_id(0)
    nsteps = pl.num_programs(0)
    nt = seq // tt

    def bounds(s, ei):
        bi = s // nt
        base = (bi * n_exp + ei) * (nt + 1) + s % nt
        return tab_ref[base], tab_ref[base + 1]

    def issue(s, slot):
        bi = s // nt
        tok0 = bi * seq + (s % nt) * tt
        for ei in range(n_exp):
            lo, hi = bounds(s, ei)
            src0 = (ei * nb + bi) * cap

            def body(p, carry):
                dst = ei * tt + idx_ref[src0 + p] - tok0
                pltpu.make_async_copy(y_ref.at[pl.ds(src0 + p, 1)], z_ref.at[slot, pl.ds(dst, 1)],
                                      sem_ref.at[slot]).start()
                return carry
            lax.fori_loop(lo, hi, body, 0)

    def count(s):
        n = jnp.int32(0)
        for ei in range(n_exp):
            lo, hi = bounds(s, ei)
            n = n + (hi - lo)
        return n

    @pl.when(step == 0)
    def _():
        z_ref[...] = jnp.zeros(z_ref.shape, z_ref.dtype)
        issue(0, 0)

    @pl.when(step + 1 < nsteps)
    def _():
        issue(step + 1, (step + 1) % 2)

    slot = step % 2
    n = count(step)

    def wait_row(p, carry):
        pltpu.make_async_copy(y_ref.at[pl.ds(0, 1)], z_ref.at[slot, pl.ds(0, 1)], sem_ref.at[slot]).wait()
        return carry

    lax.fori_loop(0, n, wait_row, 0)

    sel = sel_ref[...]
    g = g_ref[...]
    for rg in range(tt // V7X_SUBLANES):
        rows = slice(rg * V7X_SUBLANES, (rg + 1) * V7X_SUBLANES)
        acc = h_ref[rows, :]
        for ei in range(n_exp):
            m = sel[rows, ei:ei + 1] > 0.0
            zrows = slice(ei * tt + rg * V7X_SUBLANES, ei * tt + (rg + 1) * V7X_SUBLANES)
            acc = acc + jnp.where(m, z_ref[slot, zrows, :], 0.0)
        if final_norm:
            r = lax.rsqrt(jnp.mean(acc * acc, axis=-1, keepdims=True) + RMS_EPS)
            acc = (acc * r) * g
        o_ref[rows, :] = acc


def _combine(h, y, idx_flat, tab_flat, sel_t, g, e, b, s, cap, final_norm, tt=64):
    t, d = h.shape
    tt = _tile(s, tt)
    return pl.pallas_call(
        functools.partial(_combine_kernel, n_exp=e, nb=b, seq=s, cap=cap, tt=tt, final_norm=final_norm),
        grid_spec=pltpu.PrefetchScalarGridSpec(
            num_scalar_prefetch=2,
            grid=(t // tt,),
            in_specs=[
                pl.BlockSpec((tt, d), lambda i, idx, tab: (i, 0)),
                pl.BlockSpec((tt, e), lambda i, idx, tab: (i, 0)),
                pl.BlockSpec(memory_space=pl.ANY),
                pl.BlockSpec((1, d), lambda i, idx, tab: (0, 0)),
            ],
            out_specs=pl.BlockSpec((tt, d), lambda i, idx, tab: (i, 0)),
            scratch_shapes=[pltpu.VMEM((2, e * tt, d), F32), pltpu.SemaphoreType.DMA((2,))],
        ),
        out_shape=jax.ShapeDtypeStruct((t, d), F32),
        compiler_params=_params("arbitrary"),
        name="combine",
    )(idx_flat, tab_flat, h, sel_t, y, g.reshape(1, d))


def kernel(x, positions, norm1_g, w_in, conv_w, conv_b, conv_ln_g, conv_ln_b, lq1, lk1, lq2, lk2,
           subln_g, w_out, norm2_g, w_router, w_gate, w_up, w_down, final_g):
    b, s, d = x.shape
    t = b * s
    depth = norm1_g.shape[0]
    c_conv = conv_w.shape[2]
    dh = lq1.shape[1]
    dv = subln_g.shape[1]
    n_exp = w_router.shape[2]
    in_cols = w_in.shape[2]
    v_cols = w_out.shape[1] - c_conv
    heads = v_cols // dv
    qk_cols = heads * 2 * dh
    assert in_cols == 2 * c_conv + 2 * qk_cols + v_cols
    cap = CAPACITY_FACTOR * s // n_exp
    c1 = 2 * c_conv
    c3 = c1 + 2 * qk_cols

    inv_freq = ROPE_THETA ** (-jnp.arange(0, dh, 2, dtype=F32) / dh)
    ang = positions.astype(F32).reshape(t, 1) * inv_freq
    cos_t = jnp.concatenate([jnp.cos(ang), jnp.cos(ang)], axis=-1)
    sin_t = jnp.concatenate([-jnp.sin(ang), jnp.sin(ang)], axis=-1)

    xf = x.reshape(t, d)
    tt = _tile(s, 64)
    nt = s // tt
    for l in range(depth):
        lam_init = 0.8 - 0.6 * math.exp(-0.3 * l)
        xn = _rmsnorm(xf, norm1_g[l], RMS_EPS, BF16)
        glu = _glu_mm(xn, w_in[l], c_conv, 0, c_conv, False, F32)
        qk = _rope_mm(xn, w_in[l], c1, 2 * qk_cols, qk_cols, cos_t, sin_t, dh ** -0.5 * LOG2_E, dh)
        v = _mm(xn, w_in[l], c3, v_cols, BF16)

        conv_out = _conv(glu.reshape(b, s, c_conv), conv_w[l], conv_b[l], conv_ln_g[l], conv_ln_b[l])
        attn = _attention(qk, v, lq1[l], lk1[l], lq2[l], lk2[l], subln_g[l], b, s, heads, dh, dv, lam_init)
        h = _out_mm(conv_out.reshape(t, c_conv), attn, w_out[l], xf)

        aff_t = _router(h, norm2_g[l], w_router[l].T)
        idx, gates, sel, pref = _topk(aff_t, b, s, cap)
        idx_flat = idx.reshape(b, n_exp, cap).transpose(1, 0, 2).reshape(-1)
        gates_e = gates.reshape(b, n_exp, cap).transpose(1, 0, 2).reshape(n_exp, b * cap, 1)
        sel_t = sel.transpose(0, 2, 1).reshape(t, n_exp)
        tab = jnp.concatenate([jnp.zeros((b, n_exp, 1), jnp.int32), pref[:, :, tt - 1::tt]], axis=-1)

        xs = _gather_norm(h, idx_flat, norm2_g[l], n_exp, b, cap)
        act = _expert_up(xs, w_gate[l], w_up[l])
        y = _down_mm(act, w_down[l], gates_e, F32)
        xf = _combine(h, y.reshape(n_exp * b * cap, d), idx_flat, tab.reshape(-1), sel_t, final_g,
                      n_exp, b, s, cap, final_norm=(l == depth - 1), tt=tt)
    return xf.reshape(b, s, d)
```

```python
import functools
import math

import jax
import jax.numpy as jnp
from jax import lax
from jax.experimental import pallas as pl
from jax.experimental.pallas import tpu as pltpu

F32 = jnp.float32
BF16 = jnp.bfloat16

V7X_LANES = 128
V7X_SUBLANES = 8
V7X_VMEM_LIMIT_BYTES = 56 * 1024 * 1024

ROPE_THETA = 10000.0
RMS_EPS = 1e-6
LN_EPS = 1e-5
SUBLN_EPS = 1e-5
CAPACITY_FACTOR = 2
LOG2_E = math.log2(math.e)
CONV_HALO = 16


def _params(*sem):
    return pltpu.CompilerParams(dimension_semantics=sem, vmem_limit_bytes=V7X_VMEM_LIMIT_BYTES)


def _tile(n, want):
    t = min(n, want)
    assert n % t == 0, (n, want)
    return t


def _rmsnorm_kernel(x_ref, g_ref, o_ref, *, eps):
    x = x_ref[...]
    r = lax.rsqrt(jnp.mean(x * x, axis=-1, keepdims=True) + eps)
    o_ref[...] = ((x * r) * g_ref[...]).astype(o_ref.dtype)


def _rmsnorm(x, g, eps, out_dtype):
    t, d = x.shape
    tm = _tile(t, 256)
    return pl.pallas_call(
        functools.partial(_rmsnorm_kernel, eps=eps),
        grid=(t // tm,),
        in_specs=[pl.BlockSpec((tm, d), lambda i: (i, 0)), pl.BlockSpec((1, d), lambda i: (0, 0))],
        out_specs=pl.BlockSpec((tm, d), lambda i: (i, 0)),
        out_shape=jax.ShapeDtypeStruct((t, d), out_dtype),
        compiler_params=_params("parallel"),
        name="rmsnorm",
    )(x, g.reshape(1, d))


def _cast_resident(w_refs, w_scratch, row_axis):
    @pl.when(pl.program_id(row_axis) == 0)
    def _():
        for w_ref, w_s in zip(w_refs, w_scratch):
            w_s[...] = w_ref[...].astype(BF16)


def _glu_act(ua, ub, swiglu):
    if swiglu:
        return (ua * jax.nn.sigmoid(ua)) * ub
    return ua * jax.nn.sigmoid(ub)


def _glu_mm_kernel(a_ref, wa_ref, wb_ref, o_ref, wa_s, wb_s, *, swiglu):
    _cast_resident((wa_ref, wb_ref), (wa_s, wb_s), 1)
    a = a_ref[...]
    ua = jnp.dot(a, wa_s[...], preferred_element_type=F32)
    ub = jnp.dot(a, wb_s[...], preferred_element_type=F32)
    o_ref[...] = _glu_act(ua, ub, swiglu).astype(o_ref.dtype)


def _glu_mm(a, w, n, a_col, b_col, swiglu, out_dtype, tm=1024, tn=256):
    m, k = a.shape
    tm = _tile(m, tm)
    tn = _tile(n, tn)
    assert a_col % tn == 0 and b_col % tn == 0
    ao, bo = a_col // tn, b_col // tn
    return pl.pallas_call(
        functools.partial(_glu_mm_kernel, swiglu=swiglu),
        grid=(n // tn, m // tm),
        in_specs=[
            pl.BlockSpec((tm, k), lambda j, i: (i, 0)),
            pl.BlockSpec((k, tn), lambda j, i: (0, j + ao)),
            pl.BlockSpec((k, tn), lambda j, i: (0, j + bo)),
        ],
        out_specs=pl.BlockSpec((tm, tn), lambda j, i: (i, j)),
        out_shape=jax.ShapeDtypeStruct((m, n), out_dtype),
        scratch_shapes=[pltpu.VMEM((k, tn), BF16), pltpu.VMEM((k, tn), BF16)],
        compiler_params=_params("parallel", "arbitrary"),
        name="glu_matmul",
    )(a, w, w)


def _expert_up_kernel(a_ref, wg_ref, wu_ref, o_ref):
    a = a_ref[0]
    hg = jnp.dot(a, wg_ref[0].astype(BF16), preferred_element_type=F32)
    hu = jnp.dot(a, wu_ref[0].astype(BF16), preferred_element_type=F32)
    o_ref[0] = _glu_act(hg, hu, True).astype(o_ref.dtype)


def _expert_up(a, wg, wu, tn=256):
    g, m, k = a.shape
    n = wg.shape[2]
    tn = _tile(n, tn)
    return pl.pallas_call(
        _expert_up_kernel,
        grid=(g, n // tn),
        in_specs=[
            pl.BlockSpec((1, m, k), lambda e, j: (e, 0, 0), pipeline_mode=pl.Buffered(1)),
            pl.BlockSpec((1, k, tn), lambda e, j: (e, 0, j)),
            pl.BlockSpec((1, k, tn), lambda e, j: (e, 0, j)),
        ],
        out_specs=pl.BlockSpec((1, m, tn), lambda e, j: (e, 0, j)),
        out_shape=jax.ShapeDtypeStruct((g, m, n), BF16),
        compiler_params=_params("parallel", "parallel"),
        name="expert_up",
    )(a, wg, wu)


def _rope_mm_kernel(a_ref, w_ref, cos_ref, sin_ref, o_ref, w_s, *, n_q_tiles, q_scale, dh):
    _cast_resident((w_ref,), (w_s,), 1)
    u = jnp.dot(a_ref[...], w_s[...], preferred_element_type=F32)
    cos = cos_ref[...]
    sin = sin_ref[...]
    scale = jnp.where(pl.program_id(0) < n_q_tiles, q_scale, 1.0).astype(F32)
    tn = u.shape[1]
    for c in range(tn // dh):
        t = u[:, c * dh:(c + 1) * dh]
        r = t * cos + pltpu.roll(t, dh // 2, axis=1) * sin
        o_ref[:, c * dh:(c + 1) * dh] = (r * scale).astype(o_ref.dtype)


def _rope_mm(a, w, col, n, n_q, cos, sin, q_scale, dh, tm=1024, tn=512):
    m, k = a.shape
    tm = _tile(m, tm)
    tn = _tile(n_q, tn)
    assert col % tn == 0 and n % tn == 0 and tn % dh == 0
    co = col // tn
    return pl.pallas_call(
        functools.partial(_rope_mm_kernel, n_q_tiles=n_q // tn, q_scale=q_scale, dh=dh),
        grid=(n // tn, m // tm),
        in_specs=[
            pl.BlockSpec((tm, k), lambda j, i: (i, 0)),
            pl.BlockSpec((k, tn), lambda j, i: (0, j + co)),
            pl.BlockSpec((tm, dh), lambda j, i: (i, 0)),
            pl.BlockSpec((tm, dh), lambda j, i: (i, 0)),
        ],
        out_specs=pl.BlockSpec((tm, tn), lambda j, i: (i, j)),
        out_shape=jax.ShapeDtypeStruct((m, n), BF16),
        scratch_shapes=[pltpu.VMEM((k, tn), BF16)],
        compiler_params=_params("parallel", "arbitrary"),
        name="rope_matmul",
    )(a, w, cos, sin)


def _mm_kernel(a_ref, w_ref, o_ref, w_s):
    _cast_resident((w_ref,), (w_s,), 1)
    o_ref[...] = jnp.dot(a_ref[...], w_s[...], preferred_element_type=F32).astype(o_ref.dtype)


def _mm(a, w, col, n, out_dtype, tm=1024, tn=512):
    m, k = a.shape
    tm = _tile(m, tm)
    tn = _tile(n, tn)
    assert col % tn == 0
    co = col // tn
    return pl.pallas_call(
        _mm_kernel,
        grid=(n // tn, m // tm),
        in_specs=[pl.BlockSpec((tm, k), lambda j, i: (i, 0)), pl.BlockSpec((k, tn), lambda j, i: (0, j + co))],
        out_specs=pl.BlockSpec((tm, tn), lambda j, i: (i, j)),
        out_shape=jax.ShapeDtypeStruct((m, n), out_dtype),
        scratch_shapes=[pltpu.VMEM((k, tn), BF16)],
        compiler_params=_params("parallel", "arbitrary"),
        name="matmul",
    )(a, w)


def _out_mm_kernel(a1_ref, a2_ref, w1_ref, w2_ref, x_ref, o_ref, w1_s, w2_s):
    _cast_resident((w1_ref, w2_ref), (w1_s, w2_s), 1)
    acc = jnp.dot(a1_ref[...], w1_s[...], preferred_element_type=F32)
    acc = acc + jnp.dot(a2_ref[...], w2_s[...], preferred_element_type=F32)
    o_ref[...] = x_ref[...] + acc


def _out_mm(a1, a2, w, x, tm=1024, tn=512):
    m, k1 = a1.shape
    k2 = a2.shape[1]
    n = w.shape[1]
    assert k1 == k2 and w.shape[0] == k1 + k2
    tm = _tile(m, tm)
    tn = _tile(n, tn)
    return pl.pallas_call(
        _out_mm_kernel,
        grid=(n // tn, m // tm),
        in_specs=[
            pl.BlockSpec((tm, k1), lambda j, i: (i, 0)),
            pl.BlockSpec((tm, k2), lambda j, i: (i, 0)),
            pl.BlockSpec((k1, tn), lambda j, i: (0, j)),
            pl.BlockSpec((k2, tn), lambda j, i: (1, j)),
            pl.BlockSpec((tm, tn), lambda j, i: (i, j)),
        ],
        out_specs=pl.BlockSpec((tm, tn), lambda j, i: (i, j)),
        out_shape=jax.ShapeDtypeStruct((m, n), F32),
        scratch_shapes=[pltpu.VMEM((k1, tn), BF16), pltpu.VMEM((k2, tn), BF16)],
        compiler_params=_params("parallel", "arbitrary"),
        name="out_matmul",
    )(a1, a2, w, w, x)


HI16 = 0xFFFF0000


def _pack_bf16_pair(lo, hi):
    lo_bits = pltpu.bitcast(lo.astype(BF16).astype(F32), jnp.uint32)
    hi_bits = pltpu.bitcast(hi.astype(BF16).astype(F32), jnp.uint32)
    return (lo_bits >> 16) | (hi_bits & jnp.uint32(HI16))


def _unpack_bf16_pair(w):
    lo = pltpu.bitcast(w << 16, F32)
    hi = pltpu.bitcast(w & jnp.uint32(HI16), F32)
    return lo, hi


def _down_mm_kernel(a_ref, wlo_ref, whi_ref, gate_ref, o_ref):
    a = a_ref[0]
    gate = gate_ref[0]
    ylo = jnp.dot(a, wlo_ref[0].astype(BF16), preferred_element_type=F32) * gate
    yhi = jnp.dot(a, whi_ref[0].astype(BF16), preferred_element_type=F32) * gate
    o_ref[0] = _pack_bf16_pair(ylo, yhi)


def _down_mm(a, w, gates, tn=256):
    g, m, k = a.shape
    n = w.shape[2]
    half = n // 2
    tn = _tile(half, tn)
    nj = half // tn
    return pl.pallas_call(
        _down_mm_kernel,
        grid=(g, nj),
        in_specs=[
            pl.BlockSpec((1, m, k), lambda e, j: (e, 0, 0), pipeline_mode=pl.Buffered(1)),
            pl.BlockSpec((1, k, tn), lambda e, j: (e, 0, j)),
            pl.BlockSpec((1, k, tn), lambda e, j: (e, 0, j + nj)),
            pl.BlockSpec((1, m, 1), lambda e, j: (e, 0, 0)),
        ],
        out_specs=pl.BlockSpec((1, m, tn), lambda e, j: (e, 0, j)),
        out_shape=jax.ShapeDtypeStruct((g, m, half), jnp.uint32),
        compiler_params=_params("parallel", "parallel"),
        name="expert_down",
    )(a, w, w, gates)


def _conv_kernel(prev_ref, cur_ref, next_ref, cw_ref, cb_ref, lg_ref, lb_ref, o_ref,
                 win_ref, sh_ref, y_ref, *, ts, width, rc):
    i = pl.program_id(1)
    n = pl.num_programs(1)
    halo = CONV_HALO
    pad = width // 2
    c = cur_ref.shape[2]
    win_ref[0:halo, :] = jnp.where(i > 0, prev_ref[0], 0.0)
    win_ref[halo:halo + ts, :] = cur_ref[0]
    win_ref[halo + ts:halo + ts + halo, :] = jnp.where(i < n - 1, next_ref[0], 0.0)
    span = ts + 3 * V7X_SUBLANES
    for r in range(V7X_SUBLANES):
        sh_ref[r] = win_ref[r:r + span, :]

    def chunk_body(rb, carry):
        r0 = pl.multiple_of(rb * rc, rc)
        for cc in range(c // V7X_LANES):
            lanes = slice(cc * V7X_LANES, (cc + 1) * V7X_LANES)
            acc = jnp.zeros((rc, V7X_LANES), F32) + cb_ref[:, lanes]
            for w in range(width):
                off = halo - pad + w
                rows = pl.ds(pl.multiple_of(r0 + V7X_SUBLANES * (off // V7X_SUBLANES), V7X_SUBLANES), rc)
                acc = acc + sh_ref[off % V7X_SUBLANES, rows, lanes] * cw_ref[w:w + 1, lanes]
            y_ref[pl.ds(r0, rc), lanes] = acc
        return carry

    lax.fori_loop(0, ts // rc, chunk_body, 0)
    y = y_ref[...]
    mu = jnp.mean(y, axis=-1, keepdims=True)
    yc = y - mu
    var = jnp.mean(yc * yc, axis=-1, keepdims=True)
    z = (yc * lax.rsqrt(var + LN_EPS)) * lg_ref[...] + lb_ref[...]
    o_ref[0] = (z * jax.nn.sigmoid(z)).astype(o_ref.dtype)


def _conv(glu, conv_w, conv_b, ln_g, ln_b, ts=256, rc=16):
    b, s, c = glu.shape
    width = conv_w.shape[0]
    halo = CONV_HALO
    assert width // 2 < halo and c % V7X_LANES == 0
    ts = _tile(s, ts)
    rc = _tile(ts, rc)
    nh = ts // halo
    last = s // halo - 1
    return pl.pallas_call(
        functools.partial(_conv_kernel, ts=ts, width=width, rc=rc),
        grid=(b, s // ts),
        in_specs=[
            pl.BlockSpec((1, halo, c), lambda bi, i: (bi, jnp.maximum(i * nh - 1, 0), 0)),
            pl.BlockSpec((1, ts, c), lambda bi, i: (bi, i, 0)),
            pl.BlockSpec((1, halo, c), lambda bi, i: (bi, jnp.minimum((i + 1) * nh, last), 0)),
            pl.BlockSpec((width, c), lambda bi, i: (0, 0)),
            pl.BlockSpec((1, c), lambda bi, i: (0, 0)),
            pl.BlockSpec((1, c), lambda bi, i: (0, 0)),
            pl.BlockSpec((1, c), lambda bi, i: (0, 0)),
        ],
        out_specs=pl.BlockSpec((1, ts, c), lambda bi, i: (bi, i, 0)),
        out_shape=jax.ShapeDtypeStruct((b, s, c), BF16),
        scratch_shapes=[
            pltpu.VMEM((ts + 2 * halo, c), F32),
            pltpu.VMEM((V7X_SUBLANES, ts + 3 * V7X_SUBLANES, c), F32),
            pltpu.VMEM((ts, c), F32),
        ],
        compiler_params=_params("parallel", "parallel"),
        name="conformer_conv",
    )(glu, glu, glu, conv_w, conv_b.reshape(1, c), ln_g.reshape(1, c), ln_b.reshape(1, c))


def _attn_kernel(q_ref, k_ref, v_ref, lq1_ref, lk1_ref, lq2_ref, lk2_ref, g_ref, o_ref, *, dh, lam_init, kc):
    lam = (jnp.exp(jnp.sum(lq1_ref[...] * lk1_ref[...], keepdims=True))
           - jnp.exp(jnp.sum(lq2_ref[...] * lk2_ref[...], keepdims=True)) + lam_init)
    n_keys = k_ref.shape[0]

    def one_map(t):
        q = q_ref[:, t * dh:(t + 1) * dh]
        parts = []
        nck = n_keys // kc
        for c in range(nck):
            k = k_ref[c * kc:(c + 1) * kc, t * dh:(t + 1) * dh]
            s = lax.dot_general(q, k, (((1,), (1,)), ((), ())), preferred_element_type=F32)
            m = jnp.max(s, axis=-1, keepdims=True)
            p = jnp.exp2(s - m)
            l = jnp.sum(p, axis=-1, keepdims=True)
            o = jnp.dot(p.astype(BF16), v_ref[c * kc:(c + 1) * kc, :], preferred_element_type=F32)
            parts.append((m, l, o))
        m_all = functools.reduce(jnp.maximum, [m for m, _, _ in parts])
        l_all = 0.0
        o_all = 0.0
        for m, l, o in parts:
            w = jnp.exp2(m - m_all)
            l_all = l_all + l * w
            o_all = o_all + o * w
        return o_all / l_all

    o = one_map(0) - lam * one_map(1)
    r = lax.rsqrt(jnp.mean(o * o, axis=-1, keepdims=True) + SUBLN_EPS)
    o_ref[...] = (((o * r) * g_ref[...]) * (1.0 - lam_init)).astype(o_ref.dtype)


def _attention(qk, v, lq1, lk1, lq2, lk2, subln_g, b, s, heads, dh, dv, lam_init, tq=512, kc=1024):
    t = b * s
    tq = _tile(s, tq)
    kc = _tile(s, kc)
    nq = s // tq
    assert 2 * dh == dv
    lam_specs = [pl.BlockSpec((1, dh), lambda bi, h, qi: (0, 0))] * 4
    return pl.pallas_call(
        functools.partial(_attn_kernel, dh=dh, lam_init=lam_init, kc=kc),
        grid=(b, heads, nq),
        in_specs=[
            pl.BlockSpec((tq, 2 * dh), lambda bi, h, qi: (bi * nq + qi, h)),
            pl.BlockSpec((s, 2 * dh), lambda bi, h, qi: (bi, heads + h)),
            pl.BlockSpec((s, dv), lambda bi, h, qi: (bi, h)),
            *lam_specs,
            pl.BlockSpec((1, dv), lambda bi, h, qi: (0, 0)),
        ],
        out_specs=pl.BlockSpec((tq, dv), lambda bi, h, qi: (bi * nq + qi, h)),
        out_shape=jax.ShapeDtypeStruct((t, heads * dv), BF16),
        compiler_params=_params("parallel", "parallel", "parallel"),
        name="diff_attention",
    )(qk, qk, v, lq1.reshape(1, dh), lk1.reshape(1, dh), lq2.reshape(1, dh), lk2.reshape(1, dh),
      subln_g.reshape(1, dv))


def _router_kernel(h_ref, g_ref, wrt_ref, o_ref):
    h = h_ref[...]
    r = lax.rsqrt(jnp.mean(h * h, axis=-1, keepdims=True) + RMS_EPS)
    hn = (h * r) * g_ref[...]
    w = wrt_ref[...]
    hn_hi = hn.astype(BF16)
    hn_lo = (hn - hn_hi.astype(F32)).astype(BF16)
    w_hi = w.astype(BF16)
    w_lo = (w - w_hi.astype(F32)).astype(BF16)

    def nt_dot(a, bmat):
        return lax.dot_general(a, bmat, (((1,), (1,)), ((), ())), preferred_element_type=F32)

    logits = nt_dot(w_hi, hn_hi) + (nt_dot(w_hi, hn_lo) + nt_dot(w_lo, hn_hi))
    m = jnp.max(logits, axis=0, keepdims=True)
    p = jnp.exp(logits - m)
    o_ref[...] = p / jnp.sum(p, axis=0, keepdims=True)


def _router(h, g, w_router_t, tm=512):
    t, d = h.shape
    e = w_router_t.shape[0]
    tm = _tile(t, tm)
    return pl.pallas_call(
        _router_kernel,
        grid=(t // tm,),
        in_specs=[
            pl.BlockSpec((tm, d), lambda i: (i, 0)),
            pl.BlockSpec((1, d), lambda i: (0, 0)),
            pl.BlockSpec((e, d), lambda i: (0, 0)),
        ],
        out_specs=pl.BlockSpec((e, tm), lambda i: (0, i)),
        out_shape=jax.ShapeDtypeStruct((e, t), F32),
        compiler_params=_params("parallel"),
        name="router",
    )(h, g.reshape(1, d), w_router_t)


def _prefix_count(mask, tri):
    e, s = mask.shape
    carry = jnp.zeros((e, 1), F32)
    outs = []
    for c in range(s // V7X_LANES):
        chunk = mask[:, c * V7X_LANES:(c + 1) * V7X_LANES].astype(BF16)
        loc = jnp.dot(chunk, tri, preferred_element_type=F32) + carry
        outs.append(loc)
        carry = loc[:, V7X_LANES - 1:V7X_LANES]
    return jnp.concatenate(outs, axis=1)


def _topk_kernel(aff_ref, idx_ref, gate_ref, sel_ref, pref_ref, pos_ref, aff3_ref, *, cap, pc):
    b = pl.program_id(0)
    aff = aff_ref[...]
    e, s = aff.shape
    keys = pltpu.bitcast(aff, jnp.int32)

    def search(i, t):
        cand = t | jnp.left_shift(jnp.int32(1), 30 - i)
        cnt = jnp.sum(jnp.where(keys >= cand, 1.0, 0.0), axis=1, keepdims=True)
        return jnp.where(cnt >= cap, cand, t)

    thr = lax.fori_loop(0, 31, search, jnp.zeros((e, 1), jnp.int32))
    gt = keys > thr
    eq = keys == thr
    need = cap - jnp.sum(jnp.where(gt, 1.0, 0.0), axis=1, keepdims=True)
    ri = lax.broadcasted_iota(jnp.int32, (V7X_LANES, V7X_LANES), 0)
    ci = lax.broadcasted_iota(jnp.int32, (V7X_LANES, V7X_LANES), 1)
    tri = jnp.where(ri <= ci, 1.0, 0.0).astype(BF16)
    eq_rank = _prefix_count(jnp.where(eq, 1.0, 0.0), tri)
    sel = jnp.where(gt | (eq & (eq_rank <= need)), 1.0, 0.0)
    pref = _prefix_count(sel, tri)
    sel_ref[0] = sel
    pref_ref[0] = pref.astype(jnp.int32)
    pos = jnp.where(sel > 0.0, pref - 1.0, -1.0)
    for ei in range(e):
        pos_ref[ei] = pos[ei:ei + 1, :]
        aff3_ref[ei] = aff[ei:ei + 1, :]

    tok = lax.broadcasted_iota(jnp.int32, (1, s), 1)
    tok_hi = (tok // 64).astype(F32)
    tok_lo = (tok % 64).astype(F32)
    row = lax.broadcasted_iota(jnp.int32, (2 * V7X_SUBLANES, 1), 0)

    def invert(ei, carry):
        a = aff3_ref[ei]
        hi = a.astype(BF16).astype(F32)
        r1 = a - hi
        mid = r1.astype(BF16).astype(F32)
        lo = r1 - mid
        lhs = jnp.where(row == 0, tok_hi, jnp.where(row == 1, tok_lo, jnp.where(
            row == 2, hi, jnp.where(row == 3, mid, jnp.where(row == 4, lo, 0.0))))).astype(BF16)
        posrow = pos_ref[ei]
        for c in range(cap // pc):
            slot = (lax.broadcasted_iota(jnp.int32, (pc, 1), 0) + c * pc).astype(F32)
            onehot = jnp.where(slot == posrow, 1.0, 0.0).astype(BF16)
            res = lax.dot_general(lhs, onehot, (((1,), (1,)), ((), ())), preferred_element_type=F32)
            idx = res[0:1] * 64.0 + res[1:2]
            idx_ref[0, ei, :, c * pc:(c + 1) * pc] = idx.astype(jnp.int32) + b * s
            gate_ref[0, ei, :, c * pc:(c + 1) * pc] = (res[2:3] + res[3:4]) + res[4:5]
        return carry

    lax.fori_loop(0, e, invert, 0)


def _topk(aff_t, b, s, cap):
    e = aff_t.shape[0]
    pc = _tile(cap, 128)
    return pl.pallas_call(
        functools.partial(_topk_kernel, cap=cap, pc=pc),
        grid=(b,),
        in_specs=[pl.BlockSpec((e, s), lambda bi: (0, bi))],
        out_specs=[
            pl.BlockSpec((1, e, 1, cap), lambda bi: (bi, 0, 0, 0)),
            pl.BlockSpec((1, e, 1, cap), lambda bi: (bi, 0, 0, 0)),
            pl.BlockSpec((1, e, s), lambda bi: (bi, 0, 0)),
            pl.BlockSpec((1, e, s), lambda bi: (bi, 0, 0)),
        ],
        out_shape=[
            jax.ShapeDtypeStruct((b, e, 1, cap), jnp.int32),
            jax.ShapeDtypeStruct((b, e, 1, cap), F32),
            jax.ShapeDtypeStruct((b, e, s), F32),
            jax.ShapeDtypeStruct((b, e, s), jnp.int32),
        ],
        scratch_shapes=[pltpu.VMEM((e, 1, s), F32), pltpu.VMEM((e, 1, s), F32)],
        compiler_params=_params("parallel"),
        name="expert_choice_topk",
    )(aff_t)


def _gather_kernel(idx_ref, h_ref, g_ref, o_ref, buf_ref, sem_ref, *, cap):
    step = pl.program_id(0)
    nsteps = pl.num_programs(0)

    def issue(s, slot):
        def body(p, carry):
            row = idx_ref[s * cap + p]
            pltpu.make_async_copy(h_ref.at[pl.ds(row, 1)], buf_ref.at[slot, pl.ds(p, 1)],
                                  sem_ref.at[slot]).start()
            return carry
        lax.fori_loop(0, cap, body, 0, unroll=8)

    @pl.when(step == 0)
    def _():
        issue(0, 0)

    @pl.when(step + 1 < nsteps)
    def _():
        issue(step + 1, (step + 1) % 2)

    slot = step % 2
    pltpu.make_async_copy(h_ref.at[pl.ds(0, cap)], buf_ref.at[slot], sem_ref.at[slot]).wait()
    x = buf_ref[slot]
    r = lax.rsqrt(jnp.mean(x * x, axis=-1, keepdims=True) + RMS_EPS)
    o_ref[0] = ((x * r) * g_ref[...]).astype(o_ref.dtype)


def _gather_norm(h, idx_flat, g, e, b, cap):
    t, d = h.shape
    return pl.pallas_call(
        functools.partial(_gather_kernel, cap=cap),
        grid_spec=pltpu.PrefetchScalarGridSpec(
            num_scalar_prefetch=1,
            grid=(e * b,),
            in_specs=[
                pl.BlockSpec(memory_space=pl.ANY),
                pl.BlockSpec((1, d), lambda s, idx: (0, 0)),
            ],
            out_specs=pl.BlockSpec((1, cap, d), lambda s, idx: (s // b, s % b, 0)),
            scratch_shapes=[pltpu.VMEM((2, cap, d), F32), pltpu.SemaphoreType.DMA((2,))],
        ),
        out_shape=jax.ShapeDtypeStruct((e, b * cap, d), BF16),
        compiler_params=_params("arbitrary"),
        name="gather_rmsnorm",
    )(idx_flat, h, g.reshape(1, d))


def _combine_kernel(idx_ref, tab_ref, h_ref, sel_ref, y_ref, g_ref, o_ref, z_ref, sem_ref,
                    *, n_exp, nb, seq, cap, tt, final_norm):
    step = pl.program_id(0)
    nsteps = pl.num_programs(0)
    nt = seq // tt

    def bounds(s, ei):
        bi = s // nt
        base = (bi * n_exp + ei) * (nt + 1) + s % nt
        return tab_ref[base], tab_ref[base + 1]

    def issue(s, slot):
        bi = s // nt
        tok0 = bi * seq + (s % nt) * tt
        for ei in range(n_exp):
            lo, hi = bounds(s, ei)
            src0 = (ei * nb + bi) * cap

            def body(p, carry):
                dst = ei * tt + idx_ref[src0 + p] - tok0
                pltpu.make_async_copy(y_ref.at[pl.ds(src0 + p, 1)], z_ref.at[slot, pl.ds(dst, 1)],
                                      sem_ref.at[slot]).start()
                return carry
            lax.fori_loop(lo, hi, body, 0)

    def count(s):
        n = jnp.int32(0)
        for ei in range(n_exp):
            lo, hi = bounds(s, ei)
            n = n + (hi - lo)
        return n

    n_slots = z_ref.shape[0]
    ahead = n_slots - 1

    @pl.when(step == 0)
    def _():
        z_ref[...] = jnp.zeros(z_ref.shape, z_ref.dtype)
        for s0 in range(ahead):
            @pl.when(s0 < nsteps)
            def _():
                issue(s0, s0)

    @pl.when(step + ahead < nsteps)
    def _():
        issue(step + ahead, (step + ahead) % n_slots)

    slot = step % n_slots
    n = count(step)

    def wait_row(p, carry):
        pltpu.make_async_copy(y_ref.at[pl.ds(0, 1)], z_ref.at[slot, pl.ds(0, 1)], sem_ref.at[slot]).wait()
        return carry

    lax.fori_loop(0, n, wait_row, 0)

    sel = sel_ref[...]
    half = z_ref.shape[2]
    g_lo = g_ref[:, :half]
    g_hi = g_ref[:, half:]
    for rg in range(tt // V7X_SUBLANES):
        rows = slice(rg * V7X_SUBLANES, (rg + 1) * V7X_SUBLANES)
        acc_lo = h_ref[rows, :half]
        acc_hi = h_ref[rows, half:]
        for ei in range(n_exp):
            m = sel[rows, ei:ei + 1] > 0.0
            zrows = slice(ei * tt + rg * V7X_SUBLANES, ei * tt + (rg + 1) * V7X_SUBLANES)
            lo, hi = _unpack_bf16_pair(jnp.where(m, z_ref[slot, zrows, :], jnp.uint32(0)))
            acc_lo = acc_lo + lo
            acc_hi = acc_hi + hi
        if final_norm:
            ss = (jnp.sum(acc_lo * acc_lo, axis=-1, keepdims=True)
                  + jnp.sum(acc_hi * acc_hi, axis=-1, keepdims=True))
            r = lax.rsqrt(ss / (2 * half) + RMS_EPS)
            acc_lo = (acc_lo * r) * g_lo
            acc_hi = (acc_hi * r) * g_hi
        o_ref[rows, :half] = acc_lo
        o_ref[rows, half:] = acc_hi


def _combine(h, y, idx_flat, tab_flat, sel_t, g, e, b, s, cap, final_norm, tt=64, n_slots=4):
    t, d = h.shape
    assert y.shape[1] * 2 == d
    tt = _tile(s, tt)
    return pl.pallas_call(
        functools.partial(_combine_kernel, n_exp=e, nb=b, seq=s, cap=cap, tt=tt, final_norm=final_norm),
        grid_spec=pltpu.PrefetchScalarGridSpec(
            num_scalar_prefetch=2,
            grid=(t // tt,),
            in_specs=[
                pl.BlockSpec((tt, d), lambda i, idx, tab: (i, 0)),
                pl.BlockSpec((tt, e), lambda i, idx, tab: (i, 0)),
                pl.BlockSpec(memory_space=pl.ANY),
                pl.BlockSpec((1, d), lambda i, idx, tab: (0, 0)),
            ],
            out_specs=pl.BlockSpec((tt, d), lambda i, idx, tab: (i, 0)),
            scratch_shapes=[pltpu.VMEM((n_slots, e * tt, d // 2), jnp.uint32),
                            pltpu.SemaphoreType.DMA((n_slots,))],
        ),
        out_shape=jax.ShapeDtypeStruct((t, d), F32),
        compiler_params=_params("arbitrary"),
        name="combine",
    )(idx_flat, tab_flat, h, sel_t, y, g.reshape(1, d))


def kernel(x, positions, norm1_g, w_in, conv_w, conv_b, conv_ln_g, conv_ln_b, lq1, lk1, lq2, lk2,
           subln_g, w_out, norm2_g, w_router, w_gate, w_up, w_down, final_g):
    b, s, d = x.shape
    t = b * s
    depth = norm1_g.shape[0]
    c_conv = conv_w.shape[2]
    dh = lq1.shape[1]
    dv = subln_g.shape[1]
    n_exp = w_router.shape[2]
    in_cols = w_in.shape[2]
    v_cols = w_out.shape[1] - c_conv
    heads = v_cols // dv
    qk_cols = heads * 2 * dh
    assert in_cols == 2 * c_conv + 2 * qk_cols + v_cols
    cap = CAPACITY_FACTOR * s // n_exp
    c1 = 2 * c_conv
    c3 = c1 + 2 * qk_cols

    inv_freq = ROPE_THETA ** (-jnp.arange(0, dh, 2, dtype=F32) / dh)
    ang = positions.astype(F32).reshape(t, 1) * inv_freq
    cos_t = jnp.concatenate([jnp.cos(ang), jnp.cos(ang)], axis=-1)
    sin_t = jnp.concatenate([-jnp.sin(ang), jnp.sin(ang)], axis=-1)

    xf = x.reshape(t, d)
    tt = _tile(s, 64)
    nt = s // tt
    for l in range(depth):
        lam_init = 0.8 - 0.6 * math.exp(-0.3 * l)
        xn = _rmsnorm(xf, norm1_g[l], RMS_EPS, BF16)
        glu = _glu_mm(xn, w_in[l], c_conv, 0, c_conv, False, F32)
        qk = _rope_mm(xn, w_in[l], c1, 2 * qk_cols, qk_cols, cos_t, sin_t, dh ** -0.5 * LOG2_E, dh)
        v = _mm(xn, w_in[l], c3, v_cols, BF16)

        conv_out = _conv(glu.reshape(b, s, c_conv), conv_w[l], conv_b[l], conv_ln_g[l], conv_ln_b[l])
        attn = _attention(qk, v, lq1[l], lk1[l], lq2[l], lk2[l], subln_g[l], b, s, heads, dh, dv, lam_init)
        h = _out_mm(conv_out.reshape(t, c_conv), attn, w_out[l], xf)

        aff_t = _router(h, norm2_g[l], w_router[l].T)
        idx, gates, sel, pref = _topk(aff_t, b, s, cap)
        idx_flat = idx.reshape(b, n_exp, cap).transpose(1, 0, 2).reshape(-1)
        gates_e = gates.reshape(b, n_exp, cap).transpose(1, 0, 2).reshape(n_exp, b * cap, 1)
        sel_t = sel.transpose(0, 2, 1).reshape(t, n_exp)
        tab = jnp.concatenate([jnp.zeros((b, n_exp, 1), jnp.int32), pref[:, :, tt - 1::tt]], axis=-1)

        xs = _gather_norm(h, idx_flat, norm2_g[l], n_exp, b, cap)
        act = _expert_up(xs, w_gate[l], w_up[l])
        y = _down_mm(act, w_down[l], gates_e)
        xf = _combine(h, y.reshape(n_exp * b * cap, d // 2), idx_flat, tab.reshape(-1), sel_t, final_g,
                      n_exp, b, s, cap, final_norm=(l == depth - 1), tt=tt)
    return xf.reshape(b, s, d)
```

```python
import functools
import math

import jax
import jax.numpy as jnp
from jax import lax
from jax.experimental import pallas as pl
from jax.experimental.pallas import tpu as pltpu

F32 = jnp.float32
BF16 = jnp.bfloat16

V7X_LANES = 128
V7X_SUBLANES = 8
V7X_VMEM_LIMIT_BYTES = 56 * 1024 * 1024

ROPE_THETA = 10000.0
RMS_EPS = 1e-6
LN_EPS = 1e-5
SUBLN_EPS = 1e-5
CAPACITY_FACTOR = 2
LOG2_E = math.log2(math.e)
CONV_HALO = 16


def _params(*sem):
    return pltpu.CompilerParams(dimension_semantics=sem, vmem_limit_bytes=V7X_VMEM_LIMIT_BYTES)


def _tile(n, want):
    t = min(n, want)
    assert n % t == 0, (n, want)
    return t


def _rmsnorm_kernel(x_ref, g_ref, o_ref, *, eps):
    x = x_ref[...]
    r = lax.rsqrt(jnp.mean(x * x, axis=-1, keepdims=True) + eps)
    o_ref[...] = ((x * r) * g_ref[...]).astype(o_ref.dtype)


def _rmsnorm(x, g, eps, out_dtype):
    t, d = x.shape
    tm = _tile(t, 256)
    return pl.pallas_call(
        functools.partial(_rmsnorm_kernel, eps=eps),
        grid=(t // tm,),
        in_specs=[pl.BlockSpec((tm, d), lambda i: (i, 0)), pl.BlockSpec((1, d), lambda i: (0, 0))],
        out_specs=pl.BlockSpec((tm, d), lambda i: (i, 0)),
        out_shape=jax.ShapeDtypeStruct((t, d), out_dtype),
        compiler_params=_params("parallel"),
        name="rmsnorm",
    )(x, g.reshape(1, d))


def _cast_resident(w_refs, w_scratch, row_axis):
    @pl.when(pl.program_id(row_axis) == 0)
    def _():
        for w_ref, w_s in zip(w_refs, w_scratch):
            w_s[...] = w_ref[...].astype(BF16)


def _glu_act(ua, ub, swiglu):
    if swiglu:
        return (ua * jax.nn.sigmoid(ua)) * ub
    return ua * jax.nn.sigmoid(ub)


def _glu_mm_kernel(a_ref, wa_ref, wb_ref, o_ref, wa_s, wb_s, *, swiglu):
    _cast_resident((wa_ref, wb_ref), (wa_s, wb_s), 1)
    a = a_ref[...]
    ua = jnp.dot(a, wa_s[...], preferred_element_type=F32)
    ub = jnp.dot(a, wb_s[...], preferred_element_type=F32)
    o_ref[...] = _glu_act(ua, ub, swiglu).astype(o_ref.dtype)


def _glu_mm(a, w, n, a_col, b_col, swiglu, out_dtype, tm=1024, tn=256):
    m, k = a.shape
    tm = _tile(m, tm)
    tn = _tile(n, tn)
    assert a_col % tn == 0 and b_col % tn == 0
    ao, bo = a_col // tn, b_col // tn
    return pl.pallas_call(
        functools.partial(_glu_mm_kernel, swiglu=swiglu),
        grid=(n // tn, m // tm),
        in_specs=[
            pl.BlockSpec((tm, k), lambda j, i: (i, 0)),
            pl.BlockSpec((k, tn), lambda j, i: (0, j + ao)),
            pl.BlockSpec((k, tn), lambda j, i: (0, j + bo)),
        ],
        out_specs=pl.BlockSpec((tm, tn), lambda j, i: (i, j)),
        out_shape=jax.ShapeDtypeStruct((m, n), out_dtype),
        scratch_shapes=[pltpu.VMEM((k, tn), BF16), pltpu.VMEM((k, tn), BF16)],
        compiler_params=_params("parallel", "arbitrary"),
        name="glu_matmul",
    )(a, w, w)


def _expert_up_kernel(a_ref, wg_ref, wu_ref, o_ref):
    a = a_ref[0]
    hg = jnp.dot(a, wg_ref[0].astype(BF16), preferred_element_type=F32)
    hu = jnp.dot(a, wu_ref[0].astype(BF16), preferred_element_type=F32)
    o_ref[0] = _glu_act(hg, hu, True).astype(o_ref.dtype)


def _expert_up(a, wg, wu, tn=256):
    g, m, k = a.shape
    n = wg.shape[2]
    tn = _tile(n, tn)
    return pl.pallas_call(
        _expert_up_kernel,
        grid=(g, n // tn),
        in_specs=[
            pl.BlockSpec((1, m, k), lambda e, j: (e, 0, 0), pipeline_mode=pl.Buffered(1)),
            pl.BlockSpec((1, k, tn), lambda e, j: (e, 0, j)),
            pl.BlockSpec((1, k, tn), lambda e, j: (e, 0, j)),
        ],
        out_specs=pl.BlockSpec((1, m, tn), lambda e, j: (e, 0, j)),
        out_shape=jax.ShapeDtypeStruct((g, m, n), BF16),
        compiler_params=_params("parallel", "parallel"),
        name="expert_up",
    )(a, wg, wu)


def _rope_mm_kernel(a_ref, w_ref, cos_ref, sin_ref, o_ref, w_s, *, n_q_tiles, q_scale, dh):
    _cast_resident((w_ref,), (w_s,), 1)
    u = jnp.dot(a_ref[...], w_s[...], preferred_element_type=F32)
    cos = cos_ref[...]
    sin = sin_ref[...]
    scale = jnp.where(pl.program_id(0) < n_q_tiles, q_scale, 1.0).astype(F32)
    tn = u.shape[1]
    for c in range(tn // dh):
        t = u[:, c * dh:(c + 1) * dh]
        r = t * cos + pltpu.roll(t, dh // 2, axis=1) * sin
        o_ref[:, c * dh:(c + 1) * dh] = (r * scale).astype(o_ref.dtype)


def _rope_mm(a, w, col, n, n_q, cos, sin, q_scale, dh, tm=1024, tn=512):
    m, k = a.shape
    tm = _tile(m, tm)
    tn = _tile(n_q, tn)
    assert col % tn == 0 and n % tn == 0 and tn % dh == 0
    co = col // tn
    return pl.pallas_call(
        functools.partial(_rope_mm_kernel, n_q_tiles=n_q // tn, q_scale=q_scale, dh=dh),
        grid=(n // tn, m // tm),
        in_specs=[
            pl.BlockSpec((tm, k), lambda j, i: (i, 0)),
            pl.BlockSpec((k, tn), lambda j, i: (0, j + co)),
            pl.BlockSpec((tm, dh), lambda j, i: (i, 0)),
            pl.BlockSpec((tm, dh), lambda j, i: (i, 0)),
        ],
        out_specs=pl.BlockSpec((tm, tn), lambda j, i: (i, j)),
        out_shape=jax.ShapeDtypeStruct((m, n), BF16),
        scratch_shapes=[pltpu.VMEM((k, tn), BF16)],
        compiler_params=_params("parallel", "arbitrary"),
        name="rope_matmul",
    )(a, w, cos, sin)


def _mm_kernel(a_ref, w_ref, o_ref, w_s):
    _cast_resident((w_ref,), (w_s,), 1)
    o_ref[...] = jnp.dot(a_ref[...], w_s[...], preferred_element_type=F32).astype(o_ref.dtype)


def _mm(a, w, col, n, out_dtype, tm=1024, tn=512):
    m, k = a.shape
    tm = _tile(m, tm)
    tn = _tile(n, tn)
    assert col % tn == 0
    co = col // tn
    return pl.pallas_call(
        _mm_kernel,
        grid=(n // tn, m // tm),
        in_specs=[pl.BlockSpec((tm, k), lambda j, i: (i, 0)), pl.BlockSpec((k, tn), lambda j, i: (0, j + co))],
        out_specs=pl.BlockSpec((tm, tn), lambda j, i: (i, j)),
        out_shape=jax.ShapeDtypeStruct((m, n), out_dtype),
        scratch_shapes=[pltpu.VMEM((k, tn), BF16)],
        compiler_params=_params("parallel", "arbitrary"),
        name="matmul",
    )(a, w)


def _out_mm_kernel(a1_ref, a2_ref, w1_ref, w2_ref, x_ref, o_ref, w1_s, w2_s):
    _cast_resident((w1_ref, w2_ref), (w1_s, w2_s), 1)
    acc = jnp.dot(a1_ref[...], w1_s[...], preferred_element_type=F32)
    acc = acc + jnp.dot(a2_ref[...], w2_s[...], preferred_element_type=F32)
    o_ref[...] = x_ref[...] + acc


def _out_mm(a1, a2, w, x, tm=1024, tn=512):
    m, k1 = a1.shape
    k2 = a2.shape[1]
    n = w.shape[1]
    assert k1 == k2 and w.shape[0] == k1 + k2
    tm = _tile(m, tm)
    tn = _tile(n, tn)
    return pl.pallas_call(
        _out_mm_kernel,
        grid=(n // tn, m // tm),
        in_specs=[
            pl.BlockSpec((tm, k1), lambda j, i: (i, 0)),
            pl.BlockSpec((tm, k2), lambda j, i: (i, 0)),
            pl.BlockSpec((k1, tn), lambda j, i: (0, j)),
            pl.BlockSpec((k2, tn), lambda j, i: (1, j)),
            pl.BlockSpec((tm, tn), lambda j, i: (i, j)),
        ],
        out_specs=pl.BlockSpec((tm, tn), lambda j, i: (i, j)),
        out_shape=jax.ShapeDtypeStruct((m, n), F32),
        scratch_shapes=[pltpu.VMEM((k1, tn), BF16), pltpu.VMEM((k2, tn), BF16)],
        compiler_params=_params("parallel", "arbitrary"),
        name="out_matmul",
    )(a1, a2, w, w, x)


HI16 = 0xFFFF0000


def _pack_bf16_pair(lo, hi):
    lo_bits = pltpu.bitcast(lo.astype(BF16).astype(F32), jnp.uint32)
    hi_bits = pltpu.bitcast(hi.astype(BF16).astype(F32), jnp.uint32)
    return (lo_bits >> 16) | (hi_bits & jnp.uint32(HI16))


def _unpack_bf16_pair(w):
    lo = pltpu.bitcast(w << 16, F32)
    hi = pltpu.bitcast(w & jnp.uint32(HI16), F32)
    return lo, hi


def _down_mm_kernel(a_ref, wlo_ref, whi_ref, gate_ref, o_ref):
    a = a_ref[0]
    gate = gate_ref[0]
    ylo = jnp.dot(a, wlo_ref[0].astype(BF16), preferred_element_type=F32) * gate
    yhi = jnp.dot(a, whi_ref[0].astype(BF16), preferred_element_type=F32) * gate
    o_ref[0] = _pack_bf16_pair(ylo, yhi)


def _down_mm(a, w, gates, tn=256):
    g, m, k = a.shape
    n = w.shape[2]
    half = n // 2
    tn = _tile(half, tn)
    nj = half // tn
    return pl.pallas_call(
        _down_mm_kernel,
        grid=(g, nj),
        in_specs=[
            pl.BlockSpec((1, m, k), lambda e, j: (e, 0, 0), pipeline_mode=pl.Buffered(1)),
            pl.BlockSpec((1, k, tn), lambda e, j: (e, 0, j)),
            pl.BlockSpec((1, k, tn), lambda e, j: (e, 0, j + nj)),
            pl.BlockSpec((1, m, 1), lambda e, j: (e, 0, 0)),
        ],
        out_specs=pl.BlockSpec((1, m, tn), lambda e, j: (e, 0, j)),
        out_shape=jax.ShapeDtypeStruct((g, m, half), jnp.uint32),
        compiler_params=_params("parallel", "parallel"),
        name="expert_down",
    )(a, w, w, gates)


def _conv_kernel(prev_ref, cur_ref, next_ref, cw_ref, cb_ref, lg_ref, lb_ref, o_ref,
                 win_ref, sh_ref, y_ref, *, ts, width, rc):
    i = pl.program_id(1)
    n = pl.num_programs(1)
    halo = CONV_HALO
    pad = width // 2
    c = cur_ref.shape[2]
    win_ref[0:halo, :] = jnp.where(i > 0, prev_ref[0], 0.0)
    win_ref[halo:halo + ts, :] = cur_ref[0]
    win_ref[halo + ts:halo + ts + halo, :] = jnp.where(i < n - 1, next_ref[0], 0.0)
    span = ts + 3 * V7X_SUBLANES
    for r in range(V7X_SUBLANES):
        sh_ref[r] = win_ref[r:r + span, :]

    def chunk_body(rb, carry):
        r0 = pl.multiple_of(rb * rc, rc)
        for cc in range(c // V7X_LANES):
            lanes = slice(cc * V7X_LANES, (cc + 1) * V7X_LANES)
            acc = jnp.zeros((rc, V7X_LANES), F32) + cb_ref[:, lanes]
            for w in range(width):
                off = halo - pad + w
                rows = pl.ds(pl.multiple_of(r0 + V7X_SUBLANES * (off // V7X_SUBLANES), V7X_SUBLANES), rc)
                acc = acc + sh_ref[off % V7X_SUBLANES, rows, lanes] * cw_ref[w:w + 1, lanes]
            y_ref[pl.ds(r0, rc), lanes] = acc
        return carry

    lax.fori_loop(0, ts // rc, chunk_body, 0)
    y = y_ref[...]
    mu = jnp.mean(y, axis=-1, keepdims=True)
    yc = y - mu
    var = jnp.mean(yc * yc, axis=-1, keepdims=True)
    z = (yc * lax.rsqrt(var + LN_EPS)) * lg_ref[...] + lb_ref[...]
    o_ref[0] = (z * jax.nn.sigmoid(z)).astype(o_ref.dtype)


def _conv(glu, conv_w, conv_b, ln_g, ln_b, ts=256, rc=16):
    b, s, c = glu.shape
    width = conv_w.shape[0]
    halo = CONV_HALO
    assert width // 2 < halo and c % V7X_LANES == 0
    ts = _tile(s, ts)
    rc = _tile(ts, rc)
    nh = ts // halo
    last = s // halo - 1
    return pl.pallas_call(
        functools.partial(_conv_kernel, ts=ts, width=width, rc=rc),
        grid=(b, s // ts),
        in_specs=[
            pl.BlockSpec((1, halo, c), lambda bi, i: (bi, jnp.maximum(i * nh - 1, 0), 0)),
            pl.BlockSpec((1, ts, c), lambda bi, i: (bi, i, 0)),
            pl.BlockSpec((1, halo, c), lambda bi, i: (bi, jnp.minimum((i + 1) * nh, last), 0)),
            pl.BlockSpec((width, c), lambda bi, i: (0, 0)),
            pl.BlockSpec((1, c), lambda bi, i: (0, 0)),
            pl.BlockSpec((1, c), lambda bi, i: (0, 0)),
            pl.BlockSpec((1, c), lambda bi, i: (0, 0)),
        ],
        out_specs=pl.BlockSpec((1, ts, c), lambda bi, i: (bi, i, 0)),
        out_shape=jax.ShapeDtypeStruct((b, s, c), BF16),
        scratch_shapes=[
            pltpu.VMEM((ts + 2 * halo, c), F32),
            pltpu.VMEM((V7X_SUBLANES, ts + 3 * V7X_SUBLANES, c), F32),
            pltpu.VMEM((ts, c), F32),
        ],
        compiler_params=_params("parallel", "parallel"),
        name="conformer_conv",
    )(glu, glu, glu, conv_w, conv_b.reshape(1, c), ln_g.reshape(1, c), ln_b.reshape(1, c))


def _attn_kernel(q_ref, k_ref, v_ref, lq1_ref, lk1_ref, lq2_ref, lk2_ref, g_ref, o_ref, *, dh, lam_init, kc):
    lam = (jnp.exp(jnp.sum(lq1_ref[...] * lk1_ref[...], keepdims=True))
           - jnp.exp(jnp.sum(lq2_ref[...] * lk2_ref[...], keepdims=True)) + lam_init)
    n_keys = k_ref.shape[0]

    def one_map(t):
        q = q_ref[:, t * dh:(t + 1) * dh]
        parts = []
        nck = n_keys // kc
        for c in range(nck):
            k = k_ref[c * kc:(c + 1) * kc, t * dh:(t + 1) * dh]
            s = lax.dot_general(q, k, (((1,), (1,)), ((), ())), preferred_element_type=F32)
            m = jnp.max(s, axis=-1, keepdims=True)
            p = jnp.exp2(s - m)
            l = jnp.sum(p, axis=-1, keepdims=True)
            o = jnp.dot(p.astype(BF16), v_ref[c * kc:(c + 1) * kc, :], preferred_element_type=F32)
            parts.append((m, l, o))
        m_all = functools.reduce(jnp.maximum, [m for m, _, _ in parts])
        l_all = 0.0
        o_all = 0.0
        for m, l, o in parts:
            w = jnp.exp2(m - m_all)
            l_all = l_all + l * w
            o_all = o_all + o * w
        return o_all / l_all

    o = one_map(0) - lam * one_map(1)
    r = lax.rsqrt(jnp.mean(o * o, axis=-1, keepdims=True) + SUBLN_EPS)
    o_ref[...] = (((o * r) * g_ref[...]) * (1.0 - lam_init)).astype(o_ref.dtype)


def _attention(qk, v, lq1, lk1, lq2, lk2, subln_g, b, s, heads, dh, dv, lam_init, tq=512, kc=1024):
    t = b * s
    tq = _tile(s, tq)
    kc = _tile(s, kc)
    nq = s // tq
    assert 2 * dh == dv
    lam_specs = [pl.BlockSpec((1, dh), lambda bi, h, qi: (0, 0))] * 4
    return pl.pallas_call(
        functools.partial(_attn_kernel, dh=dh, lam_init=lam_init, kc=kc),
        grid=(b, heads, nq),
        in_specs=[
            pl.BlockSpec((tq, 2 * dh), lambda bi, h, qi: (bi * nq + qi, h)),
            pl.BlockSpec((s, 2 * dh), lambda bi, h, qi: (bi, heads + h)),
            pl.BlockSpec((s, dv), lambda bi, h, qi: (bi, h)),
            *lam_specs,
            pl.BlockSpec((1, dv), lambda bi, h, qi: (0, 0)),
        ],
        out_specs=pl.BlockSpec((tq, dv), lambda bi, h, qi: (bi * nq + qi, h)),
        out_shape=jax.ShapeDtypeStruct((t, heads * dv), BF16),
        compiler_params=_params("parallel", "parallel", "parallel"),
        name="diff_attention",
    )(qk, qk, v, lq1.reshape(1, dh), lk1.reshape(1, dh), lq2.reshape(1, dh), lk2.reshape(1, dh),
      subln_g.reshape(1, dv))


def _router_kernel(h_ref, g_ref, wrt_ref, o_ref):
    h = h_ref[...]
    r = lax.rsqrt(jnp.mean(h * h, axis=-1, keepdims=True) + RMS_EPS)
    hn = (h * r) * g_ref[...]
    w = wrt_ref[...]
    hn_hi = hn.astype(BF16)
    hn_lo = (hn - hn_hi.astype(F32)).astype(BF16)
    w_hi = w.astype(BF16)
    w_lo = (w - w_hi.astype(F32)).astype(BF16)

    def nt_dot(a, bmat):
        return lax.dot_general(a, bmat, (((1,), (1,)), ((), ())), preferred_element_type=F32)

    logits = nt_dot(w_hi, hn_hi) + (nt_dot(w_hi, hn_lo) + nt_dot(w_lo, hn_hi))
    m = jnp.max(logits, axis=0, keepdims=True)
    p = jnp.exp(logits - m)
    o_ref[...] = p / jnp.sum(p, axis=0, keepdims=True)


def _router(h, g, w_router_t, tm=512):
    t, d = h.shape
    e = w_router_t.shape[0]
    tm = _tile(t, tm)
    return pl.pallas_call(
        _router_kernel,
        grid=(t // tm,),
        in_specs=[
            pl.BlockSpec((tm, d), lambda i: (i, 0)),
            pl.BlockSpec((1, d), lambda i: (0, 0)),
            pl.BlockSpec((e, d), lambda i: (0, 0)),
        ],
        out_specs=pl.BlockSpec((e, tm), lambda i: (0, i)),
        out_shape=jax.ShapeDtypeStruct((e, t), F32),
        compiler_params=_params("parallel"),
        name="router",
    )(h, g.reshape(1, d), w_router_t)


def _prefix_count(mask, tri):
    e, s = mask.shape
    carry = jnp.zeros((e, 1), F32)
    outs = []
    for c in range(s // V7X_LANES):
        chunk = mask[:, c * V7X_LANES:(c + 1) * V7X_LANES].astype(BF16)
        loc = jnp.dot(chunk, tri, preferred_element_type=F32) + carry
        outs.append(loc)
        carry = loc[:, V7X_LANES - 1:V7X_LANES]
    return jnp.concatenate(outs, axis=1)


def _topk_kernel(aff_ref, idx_ref, gate_ref, sel_ref, pref_ref, pos_ref, aff3_ref, *, cap, pc):
    b = pl.program_id(0)
    aff = aff_ref[...]
    e, s = aff.shape
    keys = pltpu.bitcast(aff, jnp.int32)

    def search(i, t):
        cand = t | jnp.left_shift(jnp.int32(1), 30 - i)
        cnt = jnp.sum(jnp.where(keys >= cand, 1.0, 0.0), axis=1, keepdims=True)
        return jnp.where(cnt >= cap, cand, t)

    thr = lax.fori_loop(0, 31, search, jnp.zeros((e, 1), jnp.int32))
    gt = keys > thr
    eq = keys == thr
    need = cap - jnp.sum(jnp.where(gt, 1.0, 0.0), axis=1, keepdims=True)
    ri = lax.broadcasted_iota(jnp.int32, (V7X_LANES, V7X_LANES), 0)
    ci = lax.broadcasted_iota(jnp.int32, (V7X_LANES, V7X_LANES), 1)
    tri = jnp.where(ri <= ci, 1.0, 0.0).astype(BF16)
    eq_rank = _prefix_count(jnp.where(eq, 1.0, 0.0), tri)
    sel = jnp.where(gt | (eq & (eq_rank <= need)), 1.0, 0.0)
    pref = _prefix_count(sel, tri)
    sel_ref[0] = sel
    pref_ref[0] = pref.astype(jnp.int32)
    pos = jnp.where(sel > 0.0, pref - 1.0, -1.0)
    for ei in range(e):
        pos_ref[ei] = pos[ei:ei + 1, :]
        aff3_ref[ei] = aff[ei:ei + 1, :]

    tok = lax.broadcasted_iota(jnp.int32, (1, s), 1)
    tok_hi = (tok // 64).astype(F32)
    tok_lo = (tok % 64).astype(F32)
    row = lax.broadcasted_iota(jnp.int32, (2 * V7X_SUBLANES, 1), 0)

    def invert(ei, carry):
        a = aff3_ref[ei]
        hi = a.astype(BF16).astype(F32)
        r1 = a - hi
        mid = r1.astype(BF16).astype(F32)
        lo = r1 - mid
        lhs = jnp.where(row == 0, tok_hi, jnp.where(row == 1, tok_lo, jnp.where(
            row == 2, hi, jnp.where(row == 3, mid, jnp.where(row == 4, lo, 0.0))))).astype(BF16)
        posrow = pos_ref[ei]
        for c in range(cap // pc):
            slot = (lax.broadcasted_iota(jnp.int32, (pc, 1), 0) + c * pc).astype(F32)
            onehot = jnp.where(slot == posrow, 1.0, 0.0).astype(BF16)
            res = lax.dot_general(lhs, onehot, (((1,), (1,)), ((), ())), preferred_element_type=F32)
            idx = res[0:1] * 64.0 + res[1:2]
            idx_ref[0, ei, :, c * pc:(c + 1) * pc] = idx.astype(jnp.int32) + b * s
            gate_ref[0, ei, :, c * pc:(c + 1) * pc] = (res[2:3] + res[3:4]) + res[4:5]
        return carry

    lax.fori_loop(0, e, invert, 0)


def _topk(aff_t, b, s, cap):
    e = aff_t.shape[0]
    pc = _tile(cap, 128)
    return pl.pallas_call(
        functools.partial(_topk_kernel, cap=cap, pc=pc),
        grid=(b,),
        in_specs=[pl.BlockSpec((e, s), lambda bi: (0, bi))],
        out_specs=[
            pl.BlockSpec((1, e, 1, cap), lambda bi: (bi, 0, 0, 0)),
            pl.BlockSpec((1, e, 1, cap), lambda bi: (bi, 0, 0, 0)),
            pl.BlockSpec((1, e, s), lambda bi: (bi, 0, 0)),
            pl.BlockSpec((1, e, s), lambda bi: (bi, 0, 0)),
        ],
        out_shape=[
            jax.ShapeDtypeStruct((b, e, 1, cap), jnp.int32),
            jax.ShapeDtypeStruct((b, e, 1, cap), F32),
            jax.ShapeDtypeStruct((b, e, s), F32),
            jax.ShapeDtypeStruct((b, e, s), jnp.int32),
        ],
        scratch_shapes=[pltpu.VMEM((e, 1, s), F32), pltpu.VMEM((e, 1, s), F32)],
        compiler_params=_params("parallel"),
        name="expert_choice_topk",
    )(aff_t)


def _gather_kernel(idx_ref, h_ref, g_ref, o_ref, buf_ref, sem_ref, *, cap, rg):
    step = pl.program_id(0)
    nsteps = pl.num_programs(0)

    def issue(s, slot):
        def body(p, carry):
            row = idx_ref[s * cap + p]
            pltpu.make_async_copy(h_ref.at[pl.ds(row, 1)], buf_ref.at[slot, pl.ds(p, 1)],
                                  sem_ref.at[slot]).start()
            return carry
        lax.fori_loop(0, cap, body, 0, unroll=8)

    @pl.when(step == 0)
    def _():
        issue(0, 0)

    slot = step % 2
    nslot = 1 - slot
    nxt = jnp.where(step + 1 < nsteps, step + 1, 0)

    def wait_slot(sl):
        pltpu.make_async_copy(h_ref.at[pl.ds(0, cap)], buf_ref.at[sl], sem_ref.at[sl]).wait()

    wait_slot(slot)
    g = g_ref[...]
    for c in range(cap // rg):
        rows = slice(c * rg, (c + 1) * rg)
        x = buf_ref[slot, rows, :]
        r = lax.rsqrt(jnp.mean(x * x, axis=-1, keepdims=True) + RMS_EPS)
        o_ref[0, rows, :] = ((x * r) * g).astype(o_ref.dtype)
        for p in range(c * rg, (c + 1) * rg):
            row = idx_ref[nxt * cap + p]
            pltpu.make_async_copy(h_ref.at[pl.ds(row, 1)], buf_ref.at[nslot, pl.ds(p, 1)],
                                  sem_ref.at[nslot]).start()

    @pl.when(step == nsteps - 1)
    def _():
        wait_slot(nslot)


def _gather_norm(h, idx_flat, g, e, b, cap):
    t, d = h.shape
    return pl.pallas_call(
        functools.partial(_gather_kernel, cap=cap, rg=_tile(cap, 16)),
        grid_spec=pltpu.PrefetchScalarGridSpec(
            num_scalar_prefetch=1,
            grid=(e * b,),
            in_specs=[
                pl.BlockSpec(memory_space=pl.ANY),
                pl.BlockSpec((1, d), lambda s, idx: (0, 0)),
            ],
            out_specs=pl.BlockSpec((1, cap, d), lambda s, idx: (s // b, s % b, 0)),
            scratch_shapes=[pltpu.VMEM((2, cap, d), F32), pltpu.SemaphoreType.DMA((2,))],
        ),
        out_shape=jax.ShapeDtypeStruct((e, b * cap, d), BF16),
        compiler_params=_params("arbitrary"),
        name="gather_rmsnorm",
    )(idx_flat, h, g.reshape(1, d))


def _combine_kernel(idx_ref, tab_ref, h_ref, sel_ref, y_ref, g_ref, o_ref, z_ref, sem_ref,
                    *, n_exp, nb, seq, cap, tt, final_norm):
    step = pl.program_id(0)
    nsteps = pl.num_programs(0)
    nt = seq // tt

    def bounds(s, ei):
        bi = s // nt
        base = (bi * n_exp + ei) * (nt + 1) + s % nt
        return tab_ref[base], tab_ref[base + 1]

    def issue(s, slot):
        bi = s // nt
        tok0 = bi * seq + (s % nt) * tt
        for ei in range(n_exp):
            lo, hi = bounds(s, ei)
            src0 = (ei * nb + bi) * cap

            def body(p, carry):
                dst = ei * tt + idx_ref[src0 + p] - tok0
                pltpu.make_async_copy(y_ref.at[pl.ds(src0 + p, 1)], z_ref.at[slot, pl.ds(dst, 1)],
                                      sem_ref.at[slot]).start()
                return carry
            lax.fori_loop(lo, hi, body, 0)

        def pad_body(p, carry):
            pltpu.make_async_copy(y_ref.at[pl.ds(0, 1)], z_ref.at[slot, pl.ds(n_exp * tt + p, 1)],
                                  sem_ref.at[slot]).start()
            return carry
        lax.fori_loop(0, padded(count(s)) - count(s), pad_body, 0)

    def padded(n):
        return ((n + (V7X_SUBLANES - 1)) // V7X_SUBLANES) * V7X_SUBLANES

    def count(s):
        n = jnp.int32(0)
        for ei in range(n_exp):
            lo, hi = bounds(s, ei)
            n = n + (hi - lo)
        return n

    n_slots = z_ref.shape[0]
    ahead = n_slots - 1

    @pl.when(step == 0)
    def _():
        z_ref[...] = jnp.zeros(z_ref.shape, z_ref.dtype)
        for s0 in range(ahead):
            @pl.when(s0 < nsteps)
            def _():
                issue(s0, s0)

    @pl.when(step + ahead < nsteps)
    def _():
        issue(step + ahead, (step + ahead) % n_slots)

    slot = step % n_slots
    n = pl.multiple_of(padded(count(step)), V7X_SUBLANES)

    @pl.when(n > 0)
    def _():
        pltpu.make_async_copy(y_ref.at[pl.ds(0, n)], z_ref.at[slot, pl.ds(0, n)], sem_ref.at[slot]).wait()

    sel = sel_ref[...]
    half = z_ref.shape[2]
    g_lo = g_ref[:, :half]
    g_hi = g_ref[:, half:]
    for rg in range(tt // V7X_SUBLANES):
        rows = slice(rg * V7X_SUBLANES, (rg + 1) * V7X_SUBLANES)
        acc_lo = h_ref[rows, :half]
        acc_hi = h_ref[rows, half:]
        for ei in range(n_exp):
            m = sel[rows, ei:ei + 1] > 0.0
            zrows = slice(ei * tt + rg * V7X_SUBLANES, ei * tt + (rg + 1) * V7X_SUBLANES)
            lo, hi = _unpack_bf16_pair(jnp.where(m, z_ref[slot, zrows, :], jnp.uint32(0)))
            acc_lo = acc_lo + lo
            acc_hi = acc_hi + hi
        if final_norm:
            ss = (jnp.sum(acc_lo * acc_lo, axis=-1, keepdims=True)
                  + jnp.sum(acc_hi * acc_hi, axis=-1, keepdims=True))
            r = lax.rsqrt(ss / (2 * half) + RMS_EPS)
            acc_lo = (acc_lo * r) * g_lo
            acc_hi = (acc_hi * r) * g_hi
        o_ref[rows, :half] = acc_lo
        o_ref[rows, half:] = acc_hi


def _combine(h, y, idx_flat, tab_flat, sel_t, g, e, b, s, cap, final_norm, tt=64, n_slots=4):
    t, d = h.shape
    assert y.shape[1] * 2 == d
    tt = _tile(s, tt)
    return pl.pallas_call(
        functools.partial(_combine_kernel, n_exp=e, nb=b, seq=s, cap=cap, tt=tt, final_norm=final_norm),
        grid_spec=pltpu.PrefetchScalarGridSpec(
            num_scalar_prefetch=2,
            grid=(t // tt,),
            in_specs=[
                pl.BlockSpec((tt, d), lambda i, idx, tab: (i, 0)),
                pl.BlockSpec((tt, e), lambda i, idx, tab: (i, 0)),
                pl.BlockSpec(memory_space=pl.ANY),
                pl.BlockSpec((1, d), lambda i, idx, tab: (0, 0)),
            ],
            out_specs=pl.BlockSpec((tt, d), lambda i, idx, tab: (i, 0)),
            scratch_shapes=[pltpu.VMEM((n_slots, e * tt + V7X_SUBLANES, d // 2), jnp.uint32),
                            pltpu.SemaphoreType.DMA((n_slots,))],
        ),
        out_shape=jax.ShapeDtypeStruct((t, d), F32),
        compiler_params=_params("arbitrary"),
        name="combine",
    )(idx_flat, tab_flat, h, sel_t, y, g.reshape(1, d))


def kernel(x, positions, norm1_g, w_in, conv_w, conv_b, conv_ln_g, conv_ln_b, lq1, lk1, lq2, lk2,
           subln_g, w_out, norm2_g, w_router, w_gate, w_up, w_down, final_g):
    b, s, d = x.shape
    t = b * s
    depth = norm1_g.shape[0]
    c_conv = conv_w.shape[2]
    dh = lq1.shape[1]
    dv = subln_g.shape[1]
    n_exp = w_router.shape[2]
    in_cols = w_in.shape[2]
    v_cols = w_out.shape[1] - c_conv
    heads = v_cols // dv
    qk_cols = heads * 2 * dh
    assert in_cols == 2 * c_conv + 2 * qk_cols + v_cols
    cap = CAPACITY_FACTOR * s // n_exp
    c1 = 2 * c_conv
    c3 = c1 + 2 * qk_cols

    inv_freq = ROPE_THETA ** (-jnp.arange(0, dh, 2, dtype=F32) / dh)
    ang = positions.astype(F32).reshape(t, 1) * inv_freq
    cos_t = jnp.concatenate([jnp.cos(ang), jnp.cos(ang)], axis=-1)
    sin_t = jnp.concatenate([-jnp.sin(ang), jnp.sin(ang)], axis=-1)

    xf = x.reshape(t, d)
    tt = _tile(s, 64)
    nt = s // tt
    for l in range(depth):
        lam_init = 0.8 - 0.6 * math.exp(-0.3 * l)
        xn = _rmsnorm(xf, norm1_g[l], RMS_EPS, BF16)
        glu = _glu_mm(xn, w_in[l], c_conv, 0, c_conv, False, F32)
        qk = _rope_mm(xn, w_in[l], c1, 2 * qk_cols, qk_cols, cos_t, sin_t, dh ** -0.5 * LOG2_E, dh)
        v = _mm(xn, w_in[l], c3, v_cols, BF16)

        conv_out = _conv(glu.reshape(b, s, c_conv), conv_w[l], conv_b[l], conv_ln_g[l], conv_ln_b[l])
        attn = _attention(qk, v, lq1[l], lk1[l], lq2[l], lk2[l], subln_g[l], b, s, heads, dh, dv, lam_init)
        h = _out_mm(conv_out.reshape(t, c_conv), attn, w_out[l], xf)

        aff_t = _router(h, norm2_g[l], w_router[l].T)
        idx, gates, sel, pref = _topk(aff_t, b, s, cap)
        idx_flat = idx.reshape(b, n_exp, cap).transpose(1, 0, 2).reshape(-1)
        gates_e = gates.reshape(b, n_exp, cap).transpose(1, 0, 2).reshape(n_exp, b * cap, 1)
        sel_t = sel.transpose(0, 2, 1).reshape(t, n_exp)
        tab = jnp.concatenate([jnp.zeros((b, n_exp, 1), jnp.int32), pref[:, :, tt - 1::tt]], axis=-1)

        xs = _gather_norm(h, idx_flat, norm2_g[l], n_exp, b, cap)
        act = _expert_up(xs, w_gate[l], w_up[l])
        y = _down_mm(act, w_down[l], gates_e)
        xf = _combine(h, y.reshape(n_exp * b * cap, d // 2), idx_flat, tab.reshape(-1), sel_t, final_g,
                      n_exp, b, s, cap, final_norm=(l == depth - 1), tt=tt)
    return xf.reshape(b, s, d)
```

```python
import functools
import math

import jax
import jax.numpy as jnp
from jax import lax
from jax.experimental import pallas as pl
from jax.experimental.pallas import tpu as pltpu

F32 = jnp.float32
BF16 = jnp.bfloat16

V7X_LANES = 128
V7X_SUBLANES = 8
V7X_VMEM_LIMIT_BYTES = 56 * 1024 * 1024

ROPE_THETA = 10000.0
RMS_EPS = 1e-6
LN_EPS = 1e-5
SUBLN_EPS = 1e-5
CAPACITY_FACTOR = 2
LOG2_E = math.log2(math.e)
CONV_HALO = 16


def _params(*sem):
    return pltpu.CompilerParams(dimension_semantics=sem, vmem_limit_bytes=V7X_VMEM_LIMIT_BYTES)


def _tile(n, want):
    t = min(n, want)
    assert n % t == 0, (n, want)
    return t


def _rmsnorm_kernel(x_ref, g_ref, o_ref, *, eps):
    x = x_ref[...]
    r = lax.rsqrt(jnp.mean(x * x, axis=-1, keepdims=True) + eps)
    o_ref[...] = ((x * r) * g_ref[...]).astype(o_ref.dtype)


def _rmsnorm(x, g, eps, out_dtype):
    t, d = x.shape
    tm = _tile(t, 512)
    return pl.pallas_call(
        functools.partial(_rmsnorm_kernel, eps=eps),
        grid=(t // tm,),
        in_specs=[pl.BlockSpec((tm, d), lambda i: (i, 0)), pl.BlockSpec((1, d), lambda i: (0, 0))],
        out_specs=pl.BlockSpec((tm, d), lambda i: (i, 0)),
        out_shape=jax.ShapeDtypeStruct((t, d), out_dtype),
        compiler_params=_params("parallel"),
        name="rmsnorm",
    )(x, g.reshape(1, d))


def _cast_resident(w_refs, w_scratch, row_axis):
    @pl.when(pl.program_id(row_axis) == 0)
    def _():
        for w_ref, w_s in zip(w_refs, w_scratch):
            w_s[...] = w_ref[...].astype(BF16)


def _glu_act(ua, ub, swiglu):
    if swiglu:
        return (ua * jax.nn.sigmoid(ua)) * ub
    return ua * jax.nn.sigmoid(ub)


def _glu_mm_kernel(a_ref, wa_ref, wb_ref, o_ref, wa_s, wb_s, *, swiglu):
    _cast_resident((wa_ref, wb_ref), (wa_s, wb_s), 1)
    a = a_ref[...]
    ua = jnp.dot(a, wa_s[...], preferred_element_type=F32)
    ub = jnp.dot(a, wb_s[...], preferred_element_type=F32)
    o_ref[...] = _glu_act(ua, ub, swiglu).astype(o_ref.dtype)


def _glu_mm(a, w, n, a_col, b_col, swiglu, out_dtype, tm=1024, tn=256):
    m, k = a.shape
    tm = _tile(m, tm)
    tn = _tile(n, tn)
    assert a_col % tn == 0 and b_col % tn == 0
    ao, bo = a_col // tn, b_col // tn
    return pl.pallas_call(
        functools.partial(_glu_mm_kernel, swiglu=swiglu),
        grid=(n // tn, m // tm),
        in_specs=[
            pl.BlockSpec((tm, k), lambda j, i: (i, 0)),
            pl.BlockSpec((k, tn), lambda j, i: (0, j + ao)),
            pl.BlockSpec((k, tn), lambda j, i: (0, j + bo)),
        ],
        out_specs=pl.BlockSpec((tm, tn), lambda j, i: (i, j)),
        out_shape=jax.ShapeDtypeStruct((m, n), out_dtype),
        scratch_shapes=[pltpu.VMEM((k, tn), BF16), pltpu.VMEM((k, tn), BF16)],
        compiler_params=_params("parallel", "arbitrary"),
        name="glu_matmul",
    )(a, w, w)


def _expert_up_kernel(a_ref, wg_ref, wu_ref, o_ref):
    a = a_ref[0]
    hg = jnp.dot(a, wg_ref[0].astype(BF16), preferred_element_type=F32)
    hu = jnp.dot(a, wu_ref[0].astype(BF16), preferred_element_type=F32)
    o_ref[0] = _glu_act(hg, hu, True).astype(o_ref.dtype)


def _expert_up(a, wg, wu, tn=256):
    g, m, k = a.shape
    n = wg.shape[2]
    tn = _tile(n, tn)
    return pl.pallas_call(
        _expert_up_kernel,
        grid=(g, n // tn),
        in_specs=[
            pl.BlockSpec((1, m, k), lambda e, j: (e, 0, 0), pipeline_mode=pl.Buffered(1)),
            pl.BlockSpec((1, k, tn), lambda e, j: (e, 0, j)),
            pl.BlockSpec((1, k, tn), lambda e, j: (e, 0, j)),
        ],
        out_specs=pl.BlockSpec((1, m, tn), lambda e, j: (e, 0, j)),
        out_shape=jax.ShapeDtypeStruct((g, m, n), BF16),
        compiler_params=_params("parallel", "parallel"),
        name="expert_up",
    )(a, wg, wu)


def _rope_mm_kernel(a_ref, w_ref, cos_ref, sin_ref, o_ref, w_s, *, n_q_tiles, q_scale, dh):
    _cast_resident((w_ref,), (w_s,), 1)
    u = jnp.dot(a_ref[...], w_s[...], preferred_element_type=F32)
    cos = cos_ref[...]
    sin = sin_ref[...]
    scale = jnp.where(pl.program_id(0) < n_q_tiles, q_scale, 1.0).astype(F32)
    tn = u.shape[1]
    for c in range(tn // dh):
        t = u[:, c * dh:(c + 1) * dh]
        r = t * cos + pltpu.roll(t, dh // 2, axis=1) * sin
        o_ref[:, c * dh:(c + 1) * dh] = (r * scale).astype(o_ref.dtype)


def _rope_mm(a, w, col, n, n_q, cos, sin, q_scale, dh, tm=1024, tn=512):
    m, k = a.shape
    tm = _tile(m, tm)
    tn = _tile(n_q, tn)
    assert col % tn == 0 and n % tn == 0 and tn % dh == 0
    co = col // tn
    return pl.pallas_call(
        functools.partial(_rope_mm_kernel, n_q_tiles=n_q // tn, q_scale=q_scale, dh=dh),
        grid=(n // tn, m // tm),
        in_specs=[
            pl.BlockSpec((tm, k), lambda j, i: (i, 0)),
            pl.BlockSpec((k, tn), lambda j, i: (0, j + co)),
            pl.BlockSpec((tm, dh), lambda j, i: (i, 0)),
            pl.BlockSpec((tm, dh), lambda j, i: (i, 0)),
        ],
        out_specs=pl.BlockSpec((tm, tn), lambda j, i: (i, j)),
        out_shape=jax.ShapeDtypeStruct((m, n), BF16),
        scratch_shapes=[pltpu.VMEM((k, tn), BF16)],
        compiler_params=_params("parallel", "arbitrary"),
        name="rope_matmul",
    )(a, w, cos, sin)


def _mm_kernel(a_ref, w_ref, o_ref, w_s):
    _cast_resident((w_ref,), (w_s,), 1)
    o_ref[...] = jnp.dot(a_ref[...], w_s[...], preferred_element_type=F32).astype(o_ref.dtype)


def _mm(a, w, col, n, out_dtype, tm=1024, tn=512):
    m, k = a.shape
    tm = _tile(m, tm)
    tn = _tile(n, tn)
    assert col % tn == 0
    co = col // tn
    return pl.pallas_call(
        _mm_kernel,
        grid=(n // tn, m // tm),
        in_specs=[pl.BlockSpec((tm, k), lambda j, i: (i, 0)), pl.BlockSpec((k, tn), lambda j, i: (0, j + co))],
        out_specs=pl.BlockSpec((tm, tn), lambda j, i: (i, j)),
        out_shape=jax.ShapeDtypeStruct((m, n), out_dtype),
        scratch_shapes=[pltpu.VMEM((k, tn), BF16)],
        compiler_params=_params("parallel", "arbitrary"),
        name="matmul",
    )(a, w)


def _out_mm_kernel(a1_ref, a2_ref, w1_ref, w2_ref, x_ref, o_ref, w1_s, w2_s):
    _cast_resident((w1_ref, w2_ref), (w1_s, w2_s), 1)
    acc = jnp.dot(a1_ref[...], w1_s[...], preferred_element_type=F32)
    acc = acc + jnp.dot(a2_ref[...], w2_s[...], preferred_element_type=F32)
    o_ref[...] = x_ref[...] + acc


def _out_mm(a1, a2, w, x, tm=1024, tn=512):
    m, k1 = a1.shape
    k2 = a2.shape[1]
    n = w.shape[1]
    assert k1 == k2 and w.shape[0] == k1 + k2
    tm = _tile(m, tm)
    tn = _tile(n, tn)
    return pl.pallas_call(
        _out_mm_kernel,
        grid=(n // tn, m // tm),
        in_specs=[
            pl.BlockSpec((tm, k1), lambda j, i: (i, 0)),
            pl.BlockSpec((tm, k2), lambda j, i: (i, 0)),
            pl.BlockSpec((k1, tn), lambda j, i: (0, j)),
            pl.BlockSpec((k2, tn), lambda j, i: (1, j)),
            pl.BlockSpec((tm, tn), lambda j, i: (i, j)),
        ],
        out_specs=pl.BlockSpec((tm, tn), lambda j, i: (i, j)),
        out_shape=jax.ShapeDtypeStruct((m, n), F32),
        scratch_shapes=[pltpu.VMEM((k1, tn), BF16), pltpu.VMEM((k2, tn), BF16)],
        compiler_params=_params("parallel", "arbitrary"),
        name="out_matmul",
    )(a1, a2, w, w, x)


HI16 = 0xFFFF0000


def _pack_bf16_pair(lo, hi):
    lo_bits = pltpu.bitcast(lo.astype(BF16).astype(F32), jnp.uint32)
    hi_bits = pltpu.bitcast(hi.astype(BF16).astype(F32), jnp.uint32)
    return (lo_bits >> 16) | (hi_bits & jnp.uint32(HI16))


def _unpack_bf16_pair(w):
    lo = pltpu.bitcast(w << 16, F32)
    hi = pltpu.bitcast(w & jnp.uint32(HI16), F32)
    return lo, hi


def _down_mm_kernel(a_ref, wlo_ref, whi_ref, gate_ref, o_ref):
    a = a_ref[0]
    gate = gate_ref[0]
    ylo = jnp.dot(a, wlo_ref[0].astype(BF16), preferred_element_type=F32) * gate
    yhi = jnp.dot(a, whi_ref[0].astype(BF16), preferred_element_type=F32) * gate
    o_ref[0] = _pack_bf16_pair(ylo, yhi)


def _down_mm(a, w, gates, tn=256):
    g, m, k = a.shape
    n = w.shape[2]
    half = n // 2
    tn = _tile(half, tn)
    nj = half // tn
    return pl.pallas_call(
        _down_mm_kernel,
        grid=(g, nj),
        in_specs=[
            pl.BlockSpec((1, m, k), lambda e, j: (e, 0, 0), pipeline_mode=pl.Buffered(1)),
            pl.BlockSpec((1, k, tn), lambda e, j: (e, 0, j)),
            pl.BlockSpec((1, k, tn), lambda e, j: (e, 0, j + nj)),
            pl.BlockSpec((1, m, 1), lambda e, j: (e, 0, 0)),
        ],
        out_specs=pl.BlockSpec((1, m, tn), lambda e, j: (e, 0, j)),
        out_shape=jax.ShapeDtypeStruct((g, m, half), jnp.uint32),
        compiler_params=_params("parallel", "parallel"),
        name="expert_down",
    )(a, w, w, gates)


def _conv_kernel(prev_ref, cur_ref, next_ref, cw_ref, cb_ref, lg_ref, lb_ref, o_ref,
                 win_ref, sh_ref, y_ref, *, ts, width, rc):
    i = pl.program_id(1)
    n = pl.num_programs(1)
    halo = CONV_HALO
    pad = width // 2
    c = cur_ref.shape[2]
    win_ref[0:halo, :] = jnp.where(i > 0, prev_ref[0], 0.0)
    win_ref[halo:halo + ts, :] = cur_ref[0]
    win_ref[halo + ts:halo + ts + halo, :] = jnp.where(i < n - 1, next_ref[0], 0.0)
    span = ts + 3 * V7X_SUBLANES
    for r in range(V7X_SUBLANES):
        sh_ref[r] = win_ref[r:r + span, :]

    def chunk_body(rb, carry):
        r0 = pl.multiple_of(rb * rc, rc)
        for cc in range(c // V7X_LANES):
            lanes = slice(cc * V7X_LANES, (cc + 1) * V7X_LANES)
            acc = jnp.zeros((rc, V7X_LANES), F32) + cb_ref[:, lanes]
            for w in range(width):
                off = halo - pad + w
                rows = pl.ds(pl.multiple_of(r0 + V7X_SUBLANES * (off // V7X_SUBLANES), V7X_SUBLANES), rc)
                acc = acc + sh_ref[off % V7X_SUBLANES, rows, lanes] * cw_ref[w:w + 1, lanes]
            y_ref[pl.ds(r0, rc), lanes] = acc
        return carry

    lax.fori_loop(0, ts // rc, chunk_body, 0)
    y = y_ref[...]
    mu = jnp.mean(y, axis=-1, keepdims=True)
    yc = y - mu
    var = jnp.mean(yc * yc, axis=-1, keepdims=True)
    z = (yc * lax.rsqrt(var + LN_EPS)) * lg_ref[...] + lb_ref[...]
    o_ref[0] = (z * jax.nn.sigmoid(z)).astype(o_ref.dtype)


def _conv(glu, conv_w, conv_b, ln_g, ln_b, ts=256, rc=16):
    b, s, c = glu.shape
    width = conv_w.shape[0]
    halo = CONV_HALO
    assert width // 2 < halo and c % V7X_LANES == 0
    ts = _tile(s, ts)
    rc = _tile(ts, rc)
    nh = ts // halo
    last = s // halo - 1
    return pl.pallas_call(
        functools.partial(_conv_kernel, ts=ts, width=width, rc=rc),
        grid=(b, s // ts),
        in_specs=[
            pl.BlockSpec((1, halo, c), lambda bi, i: (bi, jnp.maximum(i * nh - 1, 0), 0)),
            pl.BlockSpec((1, ts, c), lambda bi, i: (bi, i, 0)),
            pl.BlockSpec((1, halo, c), lambda bi, i: (bi, jnp.minimum((i + 1) * nh, last), 0)),
            pl.BlockSpec((width, c), lambda bi, i: (0, 0)),
            pl.BlockSpec((1, c), lambda bi, i: (0, 0)),
            pl.BlockSpec((1, c), lambda bi, i: (0, 0)),
            pl.BlockSpec((1, c), lambda bi, i: (0, 0)),
        ],
        out_specs=pl.BlockSpec((1, ts, c), lambda bi, i: (bi, i, 0)),
        out_shape=jax.ShapeDtypeStruct((b, s, c), BF16),
        scratch_shapes=[
            pltpu.VMEM((ts + 2 * halo, c), F32),
            pltpu.VMEM((V7X_SUBLANES, ts + 3 * V7X_SUBLANES, c), F32),
            pltpu.VMEM((ts, c), F32),
        ],
        compiler_params=_params("parallel", "parallel"),
        name="conformer_conv",
    )(glu, glu, glu, conv_w, conv_b.reshape(1, c), ln_g.reshape(1, c), ln_b.reshape(1, c))


def _attn_kernel(q_ref, k_ref, v_ref, lq1_ref, lk1_ref, lq2_ref, lk2_ref, g_ref, o_ref, *, dh, lam_init, kc):
    lam = (jnp.exp(jnp.sum(lq1_ref[...] * lk1_ref[...], keepdims=True))
           - jnp.exp(jnp.sum(lq2_ref[...] * lk2_ref[...], keepdims=True)) + lam_init)
    n_keys = k_ref.shape[0]

    def one_map(t):
        q = q_ref[:, t * dh:(t + 1) * dh]
        parts = []
        nck = n_keys // kc
        for c in range(nck):
            k = k_ref[c * kc:(c + 1) * kc, t * dh:(t + 1) * dh]
            s = lax.dot_general(q, k, (((1,), (1,)), ((), ())), preferred_element_type=F32)
            m = jnp.max(s, axis=-1, keepdims=True)
            p = jnp.exp2(s - m)
            l = jnp.sum(p, axis=-1, keepdims=True)
            o = jnp.dot(p.astype(BF16), v_ref[c * kc:(c + 1) * kc, :], preferred_element_type=F32)
            parts.append((m, l, o))
        m_all = functools.reduce(jnp.maximum, [m for m, _, _ in parts])
        l_all = 0.0
        o_all = 0.0
        for m, l, o in parts:
            w = jnp.exp2(m - m_all)
            l_all = l_all + l * w
            o_all = o_all + o * w
        return o_all / l_all

    o = one_map(0) - lam * one_map(1)
    r = lax.rsqrt(jnp.mean(o * o, axis=-1, keepdims=True) + SUBLN_EPS)
    o_ref[...] = (((o * r) * g_ref[...]) * (1.0 - lam_init)).astype(o_ref.dtype)


def _attention(qk, v, lq1, lk1, lq2, lk2, subln_g, b, s, heads, dh, dv, lam_init, tq=1024, kc=1024):
    t = b * s
    tq = _tile(s, tq)
    kc = _tile(s, kc)
    nq = s // tq
    assert 2 * dh == dv
    lam_specs = [pl.BlockSpec((1, dh), lambda bi, h, qi: (0, 0))] * 4
    return pl.pallas_call(
        functools.partial(_attn_kernel, dh=dh, lam_init=lam_init, kc=kc),
        grid=(b, heads, nq),
        in_specs=[
            pl.BlockSpec((tq, 2 * dh), lambda bi, h, qi: (bi * nq + qi, h)),
            pl.BlockSpec((s, 2 * dh), lambda bi, h, qi: (bi, heads + h)),
            pl.BlockSpec((s, dv), lambda bi, h, qi: (bi, h)),
            *lam_specs,
            pl.BlockSpec((1, dv), lambda bi, h, qi: (0, 0)),
        ],
        out_specs=pl.BlockSpec((tq, dv), lambda bi, h, qi: (bi * nq + qi, h)),
        out_shape=jax.ShapeDtypeStruct((t, heads * dv), BF16),
        compiler_params=_params("parallel", "parallel", "parallel"),
        name="diff_attention",
    )(qk, qk, v, lq1.reshape(1, dh), lk1.reshape(1, dh), lq2.reshape(1, dh), lk2.reshape(1, dh),
      subln_g.reshape(1, dv))


def _router_kernel(h_ref, g_ref, wrt_ref, o_ref):
    h = h_ref[...]
    r = lax.rsqrt(jnp.mean(h * h, axis=-1, keepdims=True) + RMS_EPS)
    hn = (h * r) * g_ref[...]
    w = wrt_ref[...]
    hn_hi = hn.astype(BF16)
    hn_lo = (hn - hn_hi.astype(F32)).astype(BF16)
    w_hi = w.astype(BF16)
    w_lo = (w - w_hi.astype(F32)).astype(BF16)

    def nt_dot(a, bmat):
        return lax.dot_general(a, bmat, (((1,), (1,)), ((), ())), preferred_element_type=F32)

    logits = nt_dot(w_hi, hn_hi) + (nt_dot(w_hi, hn_lo) + nt_dot(w_lo, hn_hi))
    m = jnp.max(logits, axis=0, keepdims=True)
    p = jnp.exp(logits - m)
    o_ref[...] = p / jnp.sum(p, axis=0, keepdims=True)


def _router(h, g, w_router_t, tm=512):
    t, d = h.shape
    e = w_router_t.shape[0]
    tm = _tile(t, tm)
    return pl.pallas_call(
        _router_kernel,
        grid=(t // tm,),
        in_specs=[
            pl.BlockSpec((tm, d), lambda i: (i, 0)),
            pl.BlockSpec((1, d), lambda i: (0, 0)),
            pl.BlockSpec((e, d), lambda i: (0, 0)),
        ],
        out_specs=pl.BlockSpec((e, tm), lambda i: (0, i)),
        out_shape=jax.ShapeDtypeStruct((e, t), F32),
        compiler_params=_params("parallel"),
        name="router",
    )(h, g.reshape(1, d), w_router_t)


def _prefix_count(mask, tri):
    e, s = mask.shape
    carry = jnp.zeros((e, 1), F32)
    outs = []
    for c in range(s // V7X_LANES):
        chunk = mask[:, c * V7X_LANES:(c + 1) * V7X_LANES].astype(BF16)
        loc = jnp.dot(chunk, tri, preferred_element_type=F32) + carry
        outs.append(loc)
        carry = loc[:, V7X_LANES - 1:V7X_LANES]
    return jnp.concatenate(outs, axis=1)


def _topk_kernel(aff_ref, idx_ref, gate_ref, sel_ref, tab_ref, pos_ref, aff3_ref, *, cap, pc, tt):
    b = pl.program_id(0)
    aff = aff_ref[...]
    e, s = aff.shape
    keys = pltpu.bitcast(aff, jnp.int32)

    def search(i, t):
        cand = t | jnp.left_shift(jnp.int32(1), 30 - i)
        cnt = jnp.sum(jnp.where(keys >= cand, 1.0, 0.0), axis=1, keepdims=True)
        return jnp.where(cnt >= cap, cand, t)

    thr = lax.fori_loop(0, 31, search, jnp.zeros((e, 1), jnp.int32))
    gt = keys > thr
    eq = keys == thr
    need = cap - jnp.sum(jnp.where(gt, 1.0, 0.0), axis=1, keepdims=True)
    ri = lax.broadcasted_iota(jnp.int32, (V7X_LANES, V7X_LANES), 0)
    ci = lax.broadcasted_iota(jnp.int32, (V7X_LANES, V7X_LANES), 1)
    tri = jnp.where(ri <= ci, 1.0, 0.0).astype(BF16)
    eq_rank = _prefix_count(jnp.where(eq, 1.0, 0.0), tri)
    sel = jnp.where(gt | (eq & (eq_rank <= need)), 1.0, 0.0)
    pref = _prefix_count(sel, tri)
    sel_ref[0] = jnp.concatenate([sel, jnp.zeros((V7X_LANES - e, s), F32)], axis=0).T
    t_idx = lax.broadcasted_iota(jnp.int32, (s, V7X_LANES), 0)
    i_idx = lax.broadcasted_iota(jnp.int32, (s, V7X_LANES), 1)
    before = jnp.where(t_idx < i_idx * tt, 1.0, 0.0).astype(BF16)
    tab_ref[0] = jnp.dot(sel.astype(BF16), before, preferred_element_type=F32).astype(jnp.int32)
    pos = jnp.where(sel > 0.0, pref - 1.0, -1.0)
    for ei in range(e):
        pos_ref[ei] = pos[ei:ei + 1, :]
        aff3_ref[ei] = aff[ei:ei + 1, :]

    tok = lax.broadcasted_iota(jnp.int32, (1, s), 1)
    tok_hi = (tok // 64).astype(F32)
    tok_lo = (tok % 64).astype(F32)
    row = lax.broadcasted_iota(jnp.int32, (2 * V7X_SUBLANES, 1), 0)

    def invert(ei, carry):
        a = aff3_ref[ei]
        hi = a.astype(BF16).astype(F32)
        r1 = a - hi
        mid = r1.astype(BF16).astype(F32)
        lo = r1 - mid
        lhs = jnp.where(row == 0, tok_hi, jnp.where(row == 1, tok_lo, jnp.where(
            row == 2, hi, jnp.where(row == 3, mid, jnp.where(row == 4, lo, 0.0))))).astype(BF16)
        posrow = pos_ref[ei]
        for c in range(cap // pc):
            slot = (lax.broadcasted_iota(jnp.int32, (pc, 1), 0) + c * pc).astype(F32)
            onehot = jnp.where(slot == posrow, 1.0, 0.0).astype(BF16)
            res = lax.dot_general(lhs, onehot, (((1,), (1,)), ((), ())), preferred_element_type=F32)
            idx = res[0:1] * 64.0 + res[1:2]
            idx_ref[0, ei, :, c * pc:(c + 1) * pc] = idx.astype(jnp.int32) + b * s
            gate_ref[0, ei, :, c * pc:(c + 1) * pc] = (res[2:3] + res[3:4]) + res[4:5]
        return carry

    lax.fori_loop(0, e, invert, 0)


def _topk(aff_t, b, s, cap, tt):
    e = aff_t.shape[0]
    assert e <= V7X_LANES and s // tt + 1 <= V7X_LANES
    pc = _tile(cap, 128)
    return pl.pallas_call(
        functools.partial(_topk_kernel, cap=cap, pc=pc, tt=tt),
        grid=(b,),
        in_specs=[pl.BlockSpec((e, s), lambda bi: (0, bi))],
        out_specs=[
            pl.BlockSpec((1, e, 1, cap), lambda bi: (bi, 0, 0, 0)),
            pl.BlockSpec((1, e, 1, cap), lambda bi: (bi, 0, 0, 0)),
            pl.BlockSpec((1, s, V7X_LANES), lambda bi: (bi, 0, 0)),
            pl.BlockSpec((1, e, V7X_LANES), lambda bi: (bi, 0, 0)),
        ],
        out_shape=[
            jax.ShapeDtypeStruct((b, e, 1, cap), jnp.int32),
            jax.ShapeDtypeStruct((b, e, 1, cap), F32),
            jax.ShapeDtypeStruct((b, s, V7X_LANES), F32),
            jax.ShapeDtypeStruct((b, e, V7X_LANES), jnp.int32),
        ],
        scratch_shapes=[pltpu.VMEM((e, 1, s), F32), pltpu.VMEM((e, 1, s), F32)],
        compiler_params=_params("parallel"),
        name="expert_choice_topk",
    )(aff_t)


def _gather_kernel(idx_ref, h_ref, g_ref, o_ref, buf_ref, sem_ref, *, cap, rg):
    step = pl.program_id(0)
    nsteps = pl.num_programs(0)

    def issue(s, slot):
        def body(p, carry):
            row = idx_ref[s * cap + p]
            pltpu.make_async_copy(h_ref.at[pl.ds(row, 1)], buf_ref.at[slot, pl.ds(p, 1)],
                                  sem_ref.at[slot]).start()
            return carry
        lax.fori_loop(0, cap, body, 0, unroll=8)

    @pl.when(step == 0)
    def _():
        issue(0, 0)

    slot = step % 2
    nslot = 1 - slot
    nxt = jnp.where(step + 1 < nsteps, step + 1, 0)

    def wait_slot(sl):
        pltpu.make_async_copy(h_ref.at[pl.ds(0, cap)], buf_ref.at[sl], sem_ref.at[sl]).wait()

    wait_slot(slot)
    g = g_ref[...]
    for c in range(cap // rg):
        rows = slice(c * rg, (c + 1) * rg)
        x = buf_ref[slot, rows, :]
        r = lax.rsqrt(jnp.mean(x * x, axis=-1, keepdims=True) + RMS_EPS)
        o_ref[0, rows, :] = ((x * r) * g).astype(o_ref.dtype)
        for p in range(c * rg, (c + 1) * rg):
            row = idx_ref[nxt * cap + p]
            pltpu.make_async_copy(h_ref.at[pl.ds(row, 1)], buf_ref.at[nslot, pl.ds(p, 1)],
                                  sem_ref.at[nslot]).start()

    @pl.when(step == nsteps - 1)
    def _():
        wait_slot(nslot)


def _gather_norm(h, idx_flat, g, e, b, cap):
    t, d = h.shape
    return pl.pallas_call(
        functools.partial(_gather_kernel, cap=cap, rg=_tile(cap, 16)),
        grid_spec=pltpu.PrefetchScalarGridSpec(
            num_scalar_prefetch=1,
            grid=(e * b,),
            in_specs=[
                pl.BlockSpec(memory_space=pl.ANY),
                pl.BlockSpec((1, d), lambda s, idx: (0, 0)),
            ],
            out_specs=pl.BlockSpec((1, cap, d), lambda s, idx: (s // b, s % b, 0)),
            scratch_shapes=[pltpu.VMEM((2, cap, d), F32), pltpu.SemaphoreType.DMA((2,))],
        ),
        out_shape=jax.ShapeDtypeStruct((e, b * cap, d), BF16),
        compiler_params=_params("arbitrary"),
        name="gather_rmsnorm",
    )(idx_flat, h, g.reshape(1, d))


def _combine_kernel(idx_ref, tab_ref, h_ref, sel_ref, y_ref, g_ref, o_ref, z_ref, sem_ref,
                    *, n_exp, nb, seq, cap, tt, final_norm):
    step = pl.program_id(0)
    nsteps = pl.num_programs(0)
    nt = seq // tt

    def bounds(s, ei):
        bi = s // nt
        base = (bi * n_exp + ei) * V7X_LANES + s % nt
        return tab_ref[base], tab_ref[base + 1]

    def issue(s, slot):
        bi = s // nt
        tok0 = bi * seq + (s % nt) * tt
        for ei in range(n_exp):
            lo, hi = bounds(s, ei)
            src0 = (ei * nb + bi) * cap

            def body(p, carry):
                dst = ei * tt + idx_ref[src0 + p] - tok0
                pltpu.make_async_copy(y_ref.at[pl.ds(src0 + p, 1)], z_ref.at[slot, pl.ds(dst, 1)],
                                      sem_ref.at[slot]).start()
                return carry
            lax.fori_loop(lo, hi, body, 0)

        def pad_body(p, carry):
            pltpu.make_async_copy(y_ref.at[pl.ds(0, 1)], z_ref.at[slot, pl.ds(n_exp * tt + p, 1)],
                                  sem_ref.at[slot]).start()
            return carry
        lax.fori_loop(0, padded(count(s)) - count(s), pad_body, 0)

    def padded(n):
        return ((n + (V7X_SUBLANES - 1)) // V7X_SUBLANES) * V7X_SUBLANES

    def count(s):
        n = jnp.int32(0)
        for ei in range(n_exp):
            lo, hi = bounds(s, ei)
            n = n + (hi - lo)
        return n

    n_slots = z_ref.shape[0]
    ahead = n_slots - 1

    @pl.when(step == 0)
    def _():
        z_ref[...] = jnp.zeros(z_ref.shape, z_ref.dtype)
        for s0 in range(ahead):
            @pl.when(s0 < nsteps)
            def _():
                issue(s0, s0)

    @pl.when(step + ahead < nsteps)
    def _():
        issue(step + ahead, (step + ahead) % n_slots)

    slot = step % n_slots
    n = pl.multiple_of(padded(count(step)), V7X_SUBLANES)

    @pl.when(n > 0)
    def _():
        pltpu.make_async_copy(y_ref.at[pl.ds(0, n)], z_ref.at[slot, pl.ds(0, n)], sem_ref.at[slot]).wait()

    sel = sel_ref[...]
    half = z_ref.shape[2]
    g_lo = g_ref[:, :half]
    g_hi = g_ref[:, half:]
    for rg in range(tt // V7X_SUBLANES):
        rows = slice(rg * V7X_SUBLANES, (rg + 1) * V7X_SUBLANES)
        acc_lo = h_ref[rows, :half]
        acc_hi = h_ref[rows, half:]
        for ei in range(n_exp):
            m = sel[rows, ei:ei + 1] > 0.0
            zrows = slice(ei * tt + rg * V7X_SUBLANES, ei * tt + (rg + 1) * V7X_SUBLANES)
            lo, hi = _unpack_bf16_pair(jnp.where(m, z_ref[slot, zrows, :], jnp.uint32(0)))
            acc_lo = acc_lo + lo
            acc_hi = acc_hi + hi
        if final_norm:
            ss = (jnp.sum(acc_lo * acc_lo, axis=-1, keepdims=True)
                  + jnp.sum(acc_hi * acc_hi, axis=-1, keepdims=True))
            r = lax.rsqrt(ss / (2 * half) + RMS_EPS)
            acc_lo = (acc_lo * r) * g_lo
            acc_hi = (acc_hi * r) * g_hi
        o_ref[rows, :half] = acc_lo
        o_ref[rows, half:] = acc_hi


def _combine(h, y, idx_flat, tab_flat, sel_t, g, e, b, s, cap, final_norm, tt=64, n_slots=4):
    t, d = h.shape
    assert y.shape[1] * 2 == d
    tt = _tile(s, tt)
    return pl.pallas_call(
        functools.partial(_combine_kernel, n_exp=e, nb=b, seq=s, cap=cap, tt=tt, final_norm=final_norm),
        grid_spec=pltpu.PrefetchScalarGridSpec(
            num_scalar_prefetch=2,
            grid=(t // tt,),
            in_specs=[
                pl.BlockSpec((tt, d), lambda i, idx, tab: (i, 0)),
                pl.BlockSpec((tt, V7X_LANES), lambda i, idx, tab: (i, 0)),
                pl.BlockSpec(memory_space=pl.ANY),
                pl.BlockSpec((1, d), lambda i, idx, tab: (0, 0)),
            ],
            out_specs=pl.BlockSpec((tt, d), lambda i, idx, tab: (i, 0)),
            scratch_shapes=[pltpu.VMEM((n_slots, e * tt + V7X_SUBLANES, d // 2), jnp.uint32),
                            pltpu.SemaphoreType.DMA((n_slots,))],
        ),
        out_shape=jax.ShapeDtypeStruct((t, d), F32),
        compiler_params=_params("arbitrary"),
        name="combine",
    )(idx_flat, tab_flat, h, sel_t, y, g.reshape(1, d))


def kernel(x, positions, norm1_g, w_in, conv_w, conv_b, conv_ln_g, conv_ln_b, lq1, lk1, lq2, lk2,
           subln_g, w_out, norm2_g, w_router, w_gate, w_up, w_down, final_g):
    b, s, d = x.shape
    t = b * s
    depth = norm1_g.shape[0]
    c_conv = conv_w.shape[2]
    dh = lq1.shape[1]
    dv = subln_g.shape[1]
    n_exp = w_router.shape[2]
    in_cols = w_in.shape[2]
    v_cols = w_out.shape[1] - c_conv
    heads = v_cols // dv
    qk_cols = heads * 2 * dh
    assert in_cols == 2 * c_conv + 2 * qk_cols + v_cols
    cap = CAPACITY_FACTOR * s // n_exp
    c1 = 2 * c_conv
    c3 = c1 + 2 * qk_cols

    inv_freq = ROPE_THETA ** (-jnp.arange(0, dh, 2, dtype=F32) / dh)
    ang = positions.astype(F32).reshape(t, 1) * inv_freq
    cos_t = jnp.concatenate([jnp.cos(ang), jnp.cos(ang)], axis=-1)
    sin_t = jnp.concatenate([-jnp.sin(ang), jnp.sin(ang)], axis=-1)

    xf = x.reshape(t, d)
    tt = _tile(s, 64)
    nt = s // tt
    for l in range(depth):
        lam_init = 0.8 - 0.6 * math.exp(-0.3 * l)
        xn = _rmsnorm(xf, norm1_g[l], RMS_EPS, BF16)
        glu = _glu_mm(xn, w_in[l], c_conv, 0, c_conv, False, F32)
        qk = _rope_mm(xn, w_in[l], c1, 2 * qk_cols, qk_cols, cos_t, sin_t, dh ** -0.5 * LOG2_E, dh)
        v = _mm(xn, w_in[l], c3, v_cols, BF16)

        conv_out = _conv(glu.reshape(b, s, c_conv), conv_w[l], conv_b[l], conv_ln_g[l], conv_ln_b[l])
        attn = _attention(qk, v, lq1[l], lk1[l], lq2[l], lk2[l], subln_g[l], b, s, heads, dh, dv, lam_init)
        h = _out_mm(conv_out.reshape(t, c_conv), attn, w_out[l], xf)

        aff_t = _router(h, norm2_g[l], w_router[l].T)
        idx, gates, sel, tab = _topk(aff_t, b, s, cap, tt)
        idx_flat = idx.reshape(b, n_exp, cap).transpose(1, 0, 2).reshape(-1)
        gates_e = gates.reshape(b, n_exp, cap).transpose(1, 0, 2).reshape(n_exp, b * cap, 1)
        sel_t = sel.reshape(t, V7X_LANES)

        xs = _gather_norm(h, idx_flat, norm2_g[l], n_exp, b, cap)
        act = _expert_up(xs, w_gate[l], w_up[l])
        y = _down_mm(act, w_down[l], gates_e)
        xf = _combine(h, y.reshape(n_exp * b * cap, d // 2), idx_flat, tab.reshape(-1), sel_t, final_g,
                      n_exp, b, s, cap, final_norm=(l == depth - 1), tt=tt)
    return xf.reshape(b, s, d)
```

```python
import functools
import math

import jax
import jax.numpy as jnp
from jax import lax
from jax.experimental import pallas as pl
from jax.experimental.pallas import tpu as pltpu

F32 = jnp.float32
BF16 = jnp.bfloat16

V7X_LANES = 128
V7X_SUBLANES = 8
V7X_VMEM_LIMIT_BYTES = 56 * 1024 * 1024

ROPE_THETA = 10000.0
RMS_EPS = 1e-6
LN_EPS = 1e-5
SUBLN_EPS = 1e-5
CAPACITY_FACTOR = 2
LOG2_E = math.log2(math.e)
CONV_HALO = 16


def _params(*sem):
    return pltpu.CompilerParams(dimension_semantics=sem, vmem_limit_bytes=V7X_VMEM_LIMIT_BYTES)


def _tile(n, want):
    t = min(n, want)
    assert n % t == 0, (n, want)
    return t


def _rmsnorm_kernel(x_ref, g_ref, o_ref, *, eps):
    x = x_ref[...]
    r = lax.rsqrt(jnp.mean(x * x, axis=-1, keepdims=True) + eps)
    o_ref[...] = ((x * r) * g_ref[...]).astype(o_ref.dtype)


def _rmsnorm(x, g, eps, out_dtype):
    t, d = x.shape
    tm = _tile(t, 512)
    return pl.pallas_call(
        functools.partial(_rmsnorm_kernel, eps=eps),
        grid=(t // tm,),
        in_specs=[pl.BlockSpec((tm, d), lambda i: (i, 0)), pl.BlockSpec((1, d), lambda i: (0, 0))],
        out_specs=pl.BlockSpec((tm, d), lambda i: (i, 0)),
        out_shape=jax.ShapeDtypeStruct((t, d), out_dtype),
        compiler_params=_params("parallel"),
        name="rmsnorm",
    )(x, g.reshape(1, d))


def _cast_resident(w_refs, w_scratch, row_axis):
    @pl.when(pl.program_id(row_axis) == 0)
    def _():
        for w_ref, w_s in zip(w_refs, w_scratch):
            w_s[...] = w_ref[...].astype(BF16)


def _glu_act(ua, ub, swiglu):
    if swiglu:
        return (ua * jax.nn.sigmoid(ua)) * ub
    return ua * jax.nn.sigmoid(ub)


def _glu_mm_kernel(a_ref, wa_ref, wb_ref, o_ref, wa_s, wb_s, *, swiglu):
    _cast_resident((wa_ref, wb_ref), (wa_s, wb_s), 1)
    a = a_ref[...]
    ua = jnp.dot(a, wa_s[...], preferred_element_type=F32)
    ub = jnp.dot(a, wb_s[...], preferred_element_type=F32)
    o_ref[...] = _glu_act(ua, ub, swiglu).astype(o_ref.dtype)


def _glu_mm(a, w, n, a_col, b_col, swiglu, out_dtype, tm=1024, tn=256):
    m, k = a.shape
    tm = _tile(m, tm)
    tn = _tile(n, tn)
    assert a_col % tn == 0 and b_col % tn == 0
    ao, bo = a_col // tn, b_col // tn
    return pl.pallas_call(
        functools.partial(_glu_mm_kernel, swiglu=swiglu),
        grid=(n // tn, m // tm),
        in_specs=[
            pl.BlockSpec((tm, k), lambda j, i: (i, 0)),
            pl.BlockSpec((k, tn), lambda j, i: (0, j + ao)),
            pl.BlockSpec((k, tn), lambda j, i: (0, j + bo)),
        ],
        out_specs=pl.BlockSpec((tm, tn), lambda j, i: (i, j)),
        out_shape=jax.ShapeDtypeStruct((m, n), out_dtype),
        scratch_shapes=[pltpu.VMEM((k, tn), BF16), pltpu.VMEM((k, tn), BF16)],
        compiler_params=_params("parallel", "arbitrary"),
        name="glu_matmul",
    )(a, w, w)


def _expert_up_kernel(a_ref, wg_ref, wu_ref, o_ref):
    a = a_ref[0]
    hg = jnp.dot(a, wg_ref[0].astype(BF16), preferred_element_type=F32)
    hu = jnp.dot(a, wu_ref[0].astype(BF16), preferred_element_type=F32)
    o_ref[0] = _glu_act(hg, hu, True).astype(o_ref.dtype)


def _expert_up(a, wg, wu, tn=256):
    g, m, k = a.shape
    n = wg.shape[2]
    tn = _tile(n, tn)
    return pl.pallas_call(
        _expert_up_kernel,
        grid=(g, n // tn),
        in_specs=[
            pl.BlockSpec((1, m, k), lambda e, j: (e, 0, 0), pipeline_mode=pl.Buffered(1)),
            pl.BlockSpec((1, k, tn), lambda e, j: (e, 0, j)),
            pl.BlockSpec((1, k, tn), lambda e, j: (e, 0, j)),
        ],
        out_specs=pl.BlockSpec((1, m, tn), lambda e, j: (e, 0, j)),
        out_shape=jax.ShapeDtypeStruct((g, m, n), BF16),
        compiler_params=_params("parallel", "parallel"),
        name="expert_up",
    )(a, wg, wu)


def _rope_mm_kernel(a_ref, w_ref, cos_ref, sin_ref, o_ref, w_s, *, n_q_tiles, q_scale, dh):
    _cast_resident((w_ref,), (w_s,), 1)
    u = jnp.dot(a_ref[...], w_s[...], preferred_element_type=F32)
    cos = cos_ref[...]
    sin = sin_ref[...]
    scale = jnp.where(pl.program_id(0) < n_q_tiles, q_scale, 1.0).astype(F32)
    tn = u.shape[1]
    for c in range(tn // dh):
        t = u[:, c * dh:(c + 1) * dh]
        r = t * cos + pltpu.roll(t, dh // 2, axis=1) * sin
        o_ref[:, c * dh:(c + 1) * dh] = (r * scale).astype(o_ref.dtype)


def _rope_mm(a, w, col, n, n_q, cos, sin, q_scale, dh, tm=1024, tn=512):
    m, k = a.shape
    tm = _tile(m, tm)
    tn = _tile(n_q, tn)
    assert col % tn == 0 and n % tn == 0 and tn % dh == 0
    co = col // tn
    return pl.pallas_call(
        functools.partial(_rope_mm_kernel, n_q_tiles=n_q // tn, q_scale=q_scale, dh=dh),
        grid=(n // tn, m // tm),
        in_specs=[
            pl.BlockSpec((tm, k), lambda j, i: (i, 0)),
            pl.BlockSpec((k, tn), lambda j, i: (0, j + co)),
            pl.BlockSpec((tm, dh), lambda j, i: (i, 0)),
            pl.BlockSpec((tm, dh), lambda j, i: (i, 0)),
        ],
        out_specs=pl.BlockSpec((tm, tn), lambda j, i: (i, j)),
        out_shape=jax.ShapeDtypeStruct((m, n), BF16),
        scratch_shapes=[pltpu.VMEM((k, tn), BF16)],
        compiler_params=_params("parallel", "arbitrary"),
        name="rope_matmul",
    )(a, w, cos, sin)


def _mm_kernel(a_ref, w_ref, o_ref, w_s):
    _cast_resident((w_ref,), (w_s,), 1)
    o_ref[...] = jnp.dot(a_ref[...], w_s[...], preferred_element_type=F32).astype(o_ref.dtype)


def _mm(a, w, col, n, out_dtype, tm=1024, tn=512):
    m, k = a.shape
    tm = _tile(m, tm)
    tn = _tile(n, tn)
    assert col % tn == 0
    co = col // tn
    return pl.pallas_call(
        _mm_kernel,
        grid=(n // tn, m // tm),
        in_specs=[pl.BlockSpec((tm, k), lambda j, i: (i, 0)), pl.BlockSpec((k, tn), lambda j, i: (0, j + co))],
        out_specs=pl.BlockSpec((tm, tn), lambda j, i: (i, j)),
        out_shape=jax.ShapeDtypeStruct((m, n), out_dtype),
        scratch_shapes=[pltpu.VMEM((k, tn), BF16)],
        compiler_params=_params("parallel", "arbitrary"),
        name="matmul",
    )(a, w)


def _out_mm_kernel(a1_ref, a2_ref, w1_ref, w2_ref, x_ref, o_ref, w1_s, w2_s):
    _cast_resident((w1_ref, w2_ref), (w1_s, w2_s), 1)
    acc = jnp.dot(a1_ref[...], w1_s[...], preferred_element_type=F32)
    acc = acc + jnp.dot(a2_ref[...], w2_s[...], preferred_element_type=F32)
    o_ref[...] = x_ref[...] + acc


def _out_mm(a1, a2, w, x, tm=1024, tn=512):
    m, k1 = a1.shape
    k2 = a2.shape[1]
    n = w.shape[1]
    assert k1 == k2 and w.shape[0] == k1 + k2
    tm = _tile(m, tm)
    tn = _tile(n, tn)
    return pl.pallas_call(
        _out_mm_kernel,
        grid=(n // tn, m // tm),
        in_specs=[
            pl.BlockSpec((tm, k1), lambda j, i: (i, 0)),
            pl.BlockSpec((tm, k2), lambda j, i: (i, 0)),
            pl.BlockSpec((k1, tn), lambda j, i: (0, j)),
            pl.BlockSpec((k2, tn), lambda j, i: (1, j)),
            pl.BlockSpec((tm, tn), lambda j, i: (i, j)),
        ],
        out_specs=pl.BlockSpec((tm, tn), lambda j, i: (i, j)),
        out_shape=jax.ShapeDtypeStruct((m, n), F32),
        scratch_shapes=[pltpu.VMEM((k1, tn), BF16), pltpu.VMEM((k2, tn), BF16)],
        compiler_params=_params("parallel", "arbitrary"),
        name="out_matmul",
    )(a1, a2, w, w, x)


HI16 = 0xFFFF0000


def _pack_bf16_pair(lo, hi):
    lo_bits = pltpu.bitcast(lo.astype(BF16).astype(F32), jnp.uint32)
    hi_bits = pltpu.bitcast(hi.astype(BF16).astype(F32), jnp.uint32)
    return (lo_bits >> 16) | (hi_bits & jnp.uint32(HI16))


def _unpack_bf16_pair(w):
    lo = pltpu.bitcast(w << 16, F32)
    hi = pltpu.bitcast(w & jnp.uint32(HI16), F32)
    return lo, hi


def _down_mm_kernel(a_ref, wlo_ref, whi_ref, gate_ref, o_ref):
    a = a_ref[0]
    gate = gate_ref[0]
    ylo = jnp.dot(a, wlo_ref[0].astype(BF16), preferred_element_type=F32) * gate
    yhi = jnp.dot(a, whi_ref[0].astype(BF16), preferred_element_type=F32) * gate
    o_ref[0] = _pack_bf16_pair(ylo, yhi)


def _down_mm(a, w, gates, tn=256):
    g, m, k = a.shape
    n = w.shape[2]
    half = n // 2
    tn = _tile(half, tn)
    nj = half // tn
    return pl.pallas_call(
        _down_mm_kernel,
        grid=(g, nj),
        in_specs=[
            pl.BlockSpec((1, m, k), lambda e, j: (e, 0, 0), pipeline_mode=pl.Buffered(1)),
            pl.BlockSpec((1, k, tn), lambda e, j: (e, 0, j)),
            pl.BlockSpec((1, k, tn), lambda e, j: (e, 0, j + nj)),
            pl.BlockSpec((1, m, 1), lambda e, j: (e, 0, 0)),
        ],
        out_specs=pl.BlockSpec((1, m, tn), lambda e, j: (e, 0, j)),
        out_shape=jax.ShapeDtypeStruct((g, m, half), jnp.uint32),
        compiler_params=_params("parallel", "parallel"),
        name="expert_down",
    )(a, w, w, gates)


def _conv_kernel(prev_ref, cur_ref, next_ref, cw_ref, cb_ref, lg_ref, lb_ref, o_ref,
                 win_ref, sh_ref, y_ref, *, ts, width, rc):
    i = pl.program_id(1)
    n = pl.num_programs(1)
    halo = CONV_HALO
    pad = width // 2
    c = cur_ref.shape[2]
    win_ref[0:halo, :] = jnp.where(i > 0, prev_ref[0], 0.0)
    win_ref[halo:halo + ts, :] = cur_ref[0]
    win_ref[halo + ts:halo + ts + halo, :] = jnp.where(i < n - 1, next_ref[0], 0.0)
    span = ts + 3 * V7X_SUBLANES
    for r in range(V7X_SUBLANES):
        sh_ref[r] = win_ref[r:r + span, :]

    def chunk_body(rb, carry):
        r0 = pl.multiple_of(rb * rc, rc)
        for cc in range(c // V7X_LANES):
            lanes = slice(cc * V7X_LANES, (cc + 1) * V7X_LANES)
            acc = jnp.zeros((rc, V7X_LANES), F32) + cb_ref[:, lanes]
            for w in range(width):
                off = halo - pad + w
                rows = pl.ds(pl.multiple_of(r0 + V7X_SUBLANES * (off // V7X_SUBLANES), V7X_SUBLANES), rc)
                acc = acc + sh_ref[off % V7X_SUBLANES, rows, lanes] * cw_ref[w:w + 1, lanes]
            y_ref[pl.ds(r0, rc), lanes] = acc
        return carry

    lax.fori_loop(0, ts // rc, chunk_body, 0)
    y = y_ref[...]
    mu = jnp.mean(y, axis=-1, keepdims=True)
    yc = y - mu
    var = jnp.mean(yc * yc, axis=-1, keepdims=True)
    z = (yc * lax.rsqrt(var + LN_EPS)) * lg_ref[...] + lb_ref[...]
    o_ref[0] = (z * jax.nn.sigmoid(z)).astype(o_ref.dtype)


def _conv(glu, conv_w, conv_b, ln_g, ln_b, ts=256, rc=16):
    b, s, c = glu.shape
    width = conv_w.shape[0]
    halo = CONV_HALO
    assert width // 2 < halo and c % V7X_LANES == 0
    ts = _tile(s, ts)
    rc = _tile(ts, rc)
    nh = ts // halo
    last = s // halo - 1
    return pl.pallas_call(
        functools.partial(_conv_kernel, ts=ts, width=width, rc=rc),
        grid=(b, s // ts),
        in_specs=[
            pl.BlockSpec((1, halo, c), lambda bi, i: (bi, jnp.maximum(i * nh - 1, 0), 0)),
            pl.BlockSpec((1, ts, c), lambda bi, i: (bi, i, 0)),
            pl.BlockSpec((1, halo, c), lambda bi, i: (bi, jnp.minimum((i + 1) * nh, last), 0)),
            pl.BlockSpec((width, c), lambda bi, i: (0, 0)),
            pl.BlockSpec((1, c), lambda bi, i: (0, 0)),
            pl.BlockSpec((1, c), lambda bi, i: (0, 0)),
            pl.BlockSpec((1, c), lambda bi, i: (0, 0)),
        ],
        out_specs=pl.BlockSpec((1, ts, c), lambda bi, i: (bi, i, 0)),
        out_shape=jax.ShapeDtypeStruct((b, s, c), BF16),
        scratch_shapes=[
            pltpu.VMEM((ts + 2 * halo, c), F32),
            pltpu.VMEM((V7X_SUBLANES, ts + 3 * V7X_SUBLANES, c), F32),
            pltpu.VMEM((ts, c), F32),
        ],
        compiler_params=_params("parallel", "parallel"),
        name="conformer_conv",
    )(glu, glu, glu, conv_w, conv_b.reshape(1, c), ln_g.reshape(1, c), ln_b.reshape(1, c))


def _attn_kernel(q_ref, k_ref, v_ref, lq1_ref, lk1_ref, lq2_ref, lk2_ref, g_ref, o_ref, *, dh, lam_init, kc):
    lam = (jnp.exp(jnp.sum(lq1_ref[...] * lk1_ref[...], keepdims=True))
           - jnp.exp(jnp.sum(lq2_ref[...] * lk2_ref[...], keepdims=True)) + lam_init)
    n_keys = k_ref.shape[0]

    def one_map(t):
        q = q_ref[:, t * dh:(t + 1) * dh]
        parts = []
        nck = n_keys // kc
        for c in range(nck):
            k = k_ref[c * kc:(c + 1) * kc, t * dh:(t + 1) * dh]
            s = lax.dot_general(q, k, (((1,), (1,)), ((), ())), preferred_element_type=F32)
            m = jnp.max(s, axis=-1, keepdims=True)
            p = jnp.exp2(s - m)
            l = jnp.sum(p, axis=-1, keepdims=True)
            o = jnp.dot(p.astype(BF16), v_ref[c * kc:(c + 1) * kc, :], preferred_element_type=F32)
            parts.append((m, l, o))
        m_all = functools.reduce(jnp.maximum, [m for m, _, _ in parts])
        l_all = 0.0
        o_all = 0.0
        for m, l, o in parts:
            w = jnp.exp2(m - m_all)
            l_all = l_all + l * w
            o_all = o_all + o * w
        return o_all / l_all

    o = one_map(0) - lam * one_map(1)
    r = lax.rsqrt(jnp.mean(o * o, axis=-1, keepdims=True) + SUBLN_EPS)
    o_ref[...] = (((o * r) * g_ref[...]) * (1.0 - lam_init)).astype(o_ref.dtype)


def _attention(qk, v, lq1, lk1, lq2, lk2, subln_g, b, s, heads, dh, dv, lam_init, tq=1024, kc=1024):
    t = b * s
    tq = _tile(s, tq)
    kc = _tile(s, kc)
    nq = s // tq
    assert 2 * dh == dv
    lam_specs = [pl.BlockSpec((1, dh), lambda bi, h, qi: (0, 0))] * 4
    return pl.pallas_call(
        functools.partial(_attn_kernel, dh=dh, lam_init=lam_init, kc=kc),
        grid=(b, heads, nq),
        in_specs=[
            pl.BlockSpec((tq, 2 * dh), lambda bi, h, qi: (bi * nq + qi, h)),
            pl.BlockSpec((s, 2 * dh), lambda bi, h, qi: (bi, heads + h)),
            pl.BlockSpec((s, dv), lambda bi, h, qi: (bi, h)),
            *lam_specs,
            pl.BlockSpec((1, dv), lambda bi, h, qi: (0, 0)),
        ],
        out_specs=pl.BlockSpec((tq, dv), lambda bi, h, qi: (bi * nq + qi, h)),
        out_shape=jax.ShapeDtypeStruct((t, heads * dv), BF16),
        compiler_params=_params("parallel", "parallel", "parallel"),
        name="diff_attention",
    )(qk, qk, v, lq1.reshape(1, dh), lk1.reshape(1, dh), lq2.reshape(1, dh), lk2.reshape(1, dh),
      subln_g.reshape(1, dv))


def _router_kernel(h_ref, g_ref, wrt_ref, o_ref):
    h = h_ref[...]
    r = lax.rsqrt(jnp.mean(h * h, axis=-1, keepdims=True) + RMS_EPS)
    hn = (h * r) * g_ref[...]
    w = wrt_ref[...]
    hn_hi = hn.astype(BF16)
    hn_lo = (hn - hn_hi.astype(F32)).astype(BF16)
    w_hi = w.astype(BF16)
    w_lo = (w - w_hi.astype(F32)).astype(BF16)

    def nt_dot(a, bmat):
        return lax.dot_general(a, bmat, (((1,), (1,)), ((), ())), preferred_element_type=F32)

    logits = nt_dot(w_hi, hn_hi) + (nt_dot(w_hi, hn_lo) + nt_dot(w_lo, hn_hi))
    m = jnp.max(logits, axis=0, keepdims=True)
    p = jnp.exp(logits - m)
    o_ref[...] = p / jnp.sum(p, axis=0, keepdims=True)


def _router(h, g, w_router_t, tm=512):
    t, d = h.shape
    e = w_router_t.shape[0]
    tm = _tile(t, tm)
    return pl.pallas_call(
        _router_kernel,
        grid=(t // tm,),
        in_specs=[
            pl.BlockSpec((tm, d), lambda i: (i, 0)),
            pl.BlockSpec((1, d), lambda i: (0, 0)),
            pl.BlockSpec((e, d), lambda i: (0, 0)),
        ],
        out_specs=pl.BlockSpec((e, tm), lambda i: (0, i)),
        out_shape=jax.ShapeDtypeStruct((e, t), F32),
        compiler_params=_params("parallel"),
        name="router",
    )(h, g.reshape(1, d), w_router_t)


def _prefix_count(mask, tri):
    e, s = mask.shape
    carry = jnp.zeros((e, 1), F32)
    outs = []
    for c in range(s // V7X_LANES):
        chunk = mask[:, c * V7X_LANES:(c + 1) * V7X_LANES].astype(BF16)
        loc = jnp.dot(chunk, tri, preferred_element_type=F32) + carry
        outs.append(loc)
        carry = loc[:, V7X_LANES - 1:V7X_LANES]
    return jnp.concatenate(outs, axis=1)


def _topk_kernel(aff_ref, idx_ref, gate_ref, sel_ref, tab_ref, pos_ref, aff3_ref, *, cap, pc, tt):
    b = pl.program_id(0)
    aff = aff_ref[...]
    e, s = aff.shape
    keys = pltpu.bitcast(aff, jnp.int32)

    def search(i, t):
        cand = t | jnp.left_shift(jnp.int32(1), 30 - i)
        cnt = jnp.sum(jnp.where(keys >= cand, 1.0, 0.0), axis=1, keepdims=True)
        return jnp.where(cnt >= cap, cand, t)

    thr = lax.fori_loop(0, 31, search, jnp.zeros((e, 1), jnp.int32))
    gt = keys > thr
    eq = keys == thr
    need = cap - jnp.sum(jnp.where(gt, 1.0, 0.0), axis=1, keepdims=True)
    ri = lax.broadcasted_iota(jnp.int32, (V7X_LANES, V7X_LANES), 0)
    ci = lax.broadcasted_iota(jnp.int32, (V7X_LANES, V7X_LANES), 1)
    tri = jnp.where(ri <= ci, 1.0, 0.0).astype(BF16)
    eq_rank = _prefix_count(jnp.where(eq, 1.0, 0.0), tri)
    sel = jnp.where(gt | (eq & (eq_rank <= need)), 1.0, 0.0)
    pref = _prefix_count(sel, tri)
    sel_ref[0] = jnp.concatenate([sel, jnp.zeros((V7X_LANES - e, s), F32)], axis=0).T
    t_idx = lax.broadcasted_iota(jnp.int32, (s, V7X_LANES), 0)
    i_idx = lax.broadcasted_iota(jnp.int32, (s, V7X_LANES), 1)
    before = jnp.where(t_idx < i_idx * tt, 1.0, 0.0).astype(BF16)
    tab_ref[0] = jnp.dot(sel.astype(BF16), before, preferred_element_type=F32).astype(jnp.int32)
    pos = jnp.where(sel > 0.0, pref - 1.0, -1.0)
    for ei in range(e):
        pos_ref[ei] = pos[ei:ei + 1, :]
        aff3_ref[ei] = aff[ei:ei + 1, :]

    tok = lax.broadcasted_iota(jnp.int32, (1, s), 1)
    tok_hi = (tok // 64).astype(F32)
    tok_lo = (tok % 64).astype(F32)
    row = lax.broadcasted_iota(jnp.int32, (2 * V7X_SUBLANES, 1), 0)

    def invert(ei, carry):
        a = aff3_ref[ei]
        hi = a.astype(BF16).astype(F32)
        r1 = a - hi
        mid = r1.astype(BF16).astype(F32)
        lo = r1 - mid
        lhs = jnp.where(row == 0, tok_hi, jnp.where(row == 1, tok_lo, jnp.where(
            row == 2, hi, jnp.where(row == 3, mid, jnp.where(row == 4, lo, 0.0))))).astype(BF16)
        posrow = pos_ref[ei]
        for c in range(cap // pc):
            slot = (lax.broadcasted_iota(jnp.int32, (pc, 1), 0) + c * pc).astype(F32)
            onehot = jnp.where(slot == posrow, 1.0, 0.0).astype(BF16)
            res = lax.dot_general(lhs, onehot, (((1,), (1,)), ((), ())), preferred_element_type=F32)
            idx = res[0:1] * 64.0 + res[1:2]
            idx_ref[0, ei, :, c * pc:(c + 1) * pc] = idx.astype(jnp.int32) + b * s
            gate_ref[0, ei, :, c * pc:(c + 1) * pc] = (res[2:3] + res[3:4]) + res[4:5]
        return carry

    lax.fori_loop(0, e, invert, 0)


def _topk(aff_t, b, s, cap, tt):
    e = aff_t.shape[0]
    assert e <= V7X_LANES and s // tt + 1 <= V7X_LANES
    pc = _tile(cap, 128)
    return pl.pallas_call(
        functools.partial(_topk_kernel, cap=cap, pc=pc, tt=tt),
        grid=(b,),
        in_specs=[pl.BlockSpec((e, s), lambda bi: (0, bi))],
        out_specs=[
            pl.BlockSpec((1, e, 1, cap), lambda bi: (bi, 0, 0, 0)),
            pl.BlockSpec((1, e, 1, cap), lambda bi: (bi, 0, 0, 0)),
            pl.BlockSpec((1, s, V7X_LANES), lambda bi: (bi, 0, 0)),
            pl.BlockSpec((1, e, V7X_LANES), lambda bi: (bi, 0, 0)),
        ],
        out_shape=[
            jax.ShapeDtypeStruct((b, e, 1, cap), jnp.int32),
            jax.ShapeDtypeStruct((b, e, 1, cap), F32),
            jax.ShapeDtypeStruct((b, s, V7X_LANES), F32),
            jax.ShapeDtypeStruct((b, e, V7X_LANES), jnp.int32),
        ],
        scratch_shapes=[pltpu.VMEM((e, 1, s), F32), pltpu.VMEM((e, 1, s), F32)],
        compiler_params=_params("parallel"),
        name="expert_choice_topk",
    )(aff_t)


def _gather_kernel(idx_ref, h_ref, g_ref, o_ref, buf_ref, sem_ref, *, cap, rg):
    step = pl.program_id(0)
    nsteps = pl.num_programs(0)

    def issue(s, slot):
        def body(p, carry):
            row = idx_ref[s * cap + p]
            pltpu.make_async_copy(h_ref.at[pl.ds(row, 1)], buf_ref.at[slot, pl.ds(p, 1)],
                                  sem_ref.at[slot]).start()
            return carry
        lax.fori_loop(0, cap, body, 0, unroll=8)

    @pl.when(step == 0)
    def _():
        issue(0, 0)

    slot = step % 2
    nslot = 1 - slot
    nxt = jnp.where(step + 1 < nsteps, step + 1, 0)

    def wait_slot(sl):
        pltpu.make_async_copy(h_ref.at[pl.ds(0, cap)], buf_ref.at[sl], sem_ref.at[sl]).wait()

    wait_slot(slot)
    g = g_ref[...]
    for c in range(cap // rg):
        rows = slice(c * rg, (c + 1) * rg)
        x = buf_ref[slot, rows, :]
        r = lax.rsqrt(jnp.mean(x * x, axis=-1, keepdims=True) + RMS_EPS)
        o_ref[0, rows, :] = ((x * r) * g).astype(o_ref.dtype)
        for p in range(c * rg, (c + 1) * rg):
            row = idx_ref[nxt * cap + p]
            pltpu.make_async_copy(h_ref.at[pl.ds(row, 1)], buf_ref.at[nslot, pl.ds(p, 1)],
                                  sem_ref.at[nslot]).start()

    @pl.when(step == nsteps - 1)
    def _():
        wait_slot(nslot)


def _gather_norm(h, idx_flat, g, e, b, cap):
    t, d = h.shape
    return pl.pallas_call(
        functools.partial(_gather_kernel, cap=cap, rg=_tile(cap, 16)),
        grid_spec=pltpu.PrefetchScalarGridSpec(
            num_scalar_prefetch=1,
            grid=(e * b,),
            in_specs=[
                pl.BlockSpec(memory_space=pl.ANY),
                pl.BlockSpec((1, d), lambda s, idx: (0, 0)),
            ],
            out_specs=pl.BlockSpec((1, cap, d), lambda s, idx: (s // b, s % b, 0)),
            scratch_shapes=[pltpu.VMEM((2, cap, d), F32), pltpu.SemaphoreType.DMA((2,))],
        ),
        out_shape=jax.ShapeDtypeStruct((e, b * cap, d), BF16),
        compiler_params=_params("arbitrary"),
        name="gather_rmsnorm",
    )(idx_flat, h, g.reshape(1, d))


def _combine_kernel(idx_ref, tab_ref, h_ref, sel_ref, y_ref, g_ref, o_ref, z_ref, sem_ref,
                    *, n_exp, nb, seq, cap, tt, final_norm):
    step = pl.program_id(0)
    nsteps = pl.num_programs(0)
    nt = seq // tt

    def bounds(s, ei):
        bi = s // nt
        base = (bi * n_exp + ei) * V7X_LANES + s % nt
        return tab_ref[base], tab_ref[base + 1]

    def issue(s, slot):
        bi = s // nt
        tok0 = bi * seq + (s % nt) * tt
        for ei in range(n_exp):
            lo, hi = bounds(s, ei)
            src0 = (ei * nb + bi) * cap

            def body(p, carry):
                dst = ei * tt + idx_ref[src0 + p] - tok0
                pltpu.make_async_copy(y_ref.at[pl.ds(src0 + p, 1)], z_ref.at[slot, pl.ds(dst, 1)],
                                      sem_ref.at[slot]).start()
                return carry
            lax.fori_loop(lo, hi, body, 0)

        def pad_body(p, carry):
            pltpu.make_async_copy(y_ref.at[pl.ds(0, 1)], z_ref.at[slot, pl.ds(n_exp * tt + p, 1)],
                                  sem_ref.at[slot]).start()
            return carry
        lax.fori_loop(0, padded(count(s)) - count(s), pad_body, 0)

    def padded(n):
        return ((n + (V7X_SUBLANES - 1)) // V7X_SUBLANES) * V7X_SUBLANES

    def count(s):
        n = jnp.int32(0)
        for ei in range(n_exp):
            lo, hi = bounds(s, ei)
            n = n + (hi - lo)
        return n

    n_slots = z_ref.shape[0]
    ahead = n_slots - 1

    @pl.when(step == 0)
    def _():
        z_ref[...] = jnp.zeros(z_ref.shape, z_ref.dtype)
        for s0 in range(ahead):
            @pl.when(s0 < nsteps)
            def _():
                issue(s0, s0)

    @pl.when(step + ahead < nsteps)
    def _():
        issue(step + ahead, (step + ahead) % n_slots)

    slot = step % n_slots
    n = pl.multiple_of(padded(count(step)), V7X_SUBLANES)

    @pl.when(n > 0)
    def _():
        pltpu.make_async_copy(y_ref.at[pl.ds(0, n)], z_ref.at[slot, pl.ds(0, n)], sem_ref.at[slot]).wait()

    sel = sel_ref[...]
    half = z_ref.shape[2]
    g_lo = g_ref[:, :half]
    g_hi = g_ref[:, half:]
    for rg in range(tt // V7X_SUBLANES):
        rows = slice(rg * V7X_SUBLANES, (rg + 1) * V7X_SUBLANES)
        acc_lo = h_ref[rows, :half]
        acc_hi = h_ref[rows, half:]
        for ei in range(n_exp):
            m = sel[rows, ei:ei + 1] > 0.0
            zrows = slice(ei * tt + rg * V7X_SUBLANES, ei * tt + (rg + 1) * V7X_SUBLANES)
            lo, hi = _unpack_bf16_pair(jnp.where(m, z_ref[slot, zrows, :], jnp.uint32(0)))
            acc_lo = acc_lo + lo
            acc_hi = acc_hi + hi
        if final_norm:
            ss = (jnp.sum(acc_lo * acc_lo, axis=-1, keepdims=True)
                  + jnp.sum(acc_hi * acc_hi, axis=-1, keepdims=True))
            r = lax.rsqrt(ss / (2 * half) + RMS_EPS)
            acc_lo = (acc_lo * r) * g_lo
            acc_hi = (acc_hi * r) * g_hi
        o_ref[rows, :half] = acc_lo
        o_ref[rows, half:] = acc_hi


def _combine(h, y, idx_flat, tab_flat, sel_t, g, e, b, s, cap, final_norm, tt=128, n_slots=2):
    t, d = h.shape
    assert y.shape[1] * 2 == d
    tt = _tile(s, tt)
    return pl.pallas_call(
        functools.partial(_combine_kernel, n_exp=e, nb=b, seq=s, cap=cap, tt=tt, final_norm=final_norm),
        grid_spec=pltpu.PrefetchScalarGridSpec(
            num_scalar_prefetch=2,
            grid=(t // tt,),
            in_specs=[
                pl.BlockSpec((tt, d), lambda i, idx, tab: (i, 0)),
                pl.BlockSpec((tt, V7X_LANES), lambda i, idx, tab: (i, 0)),
                pl.BlockSpec(memory_space=pl.ANY),
                pl.BlockSpec((1, d), lambda i, idx, tab: (0, 0)),
            ],
            out_specs=pl.BlockSpec((tt, d), lambda i, idx, tab: (i, 0)),
            scratch_shapes=[pltpu.VMEM((n_slots, e * tt + V7X_SUBLANES, d // 2), jnp.uint32),
                            pltpu.SemaphoreType.DMA((n_slots,))],
        ),
        out_shape=jax.ShapeDtypeStruct((t, d), F32),
        compiler_params=_params("arbitrary"),
        name="combine",
    )(idx_flat, tab_flat, h, sel_t, y, g.reshape(1, d))


def kernel(x, positions, norm1_g, w_in, conv_w, conv_b, conv_ln_g, conv_ln_b, lq1, lk1, lq2, lk2,
           subln_g, w_out, norm2_g, w_router, w_gate, w_up, w_down, final_g):
    b, s, d = x.shape
    t = b * s
    depth = norm1_g.shape[0]
    c_conv = conv_w.shape[2]
    dh = lq1.shape[1]
    dv = subln_g.shape[1]
    n_exp = w_router.shape[2]
    in_cols = w_in.shape[2]
    v_cols = w_out.shape[1] - c_conv
    heads = v_cols // dv
    qk_cols = heads * 2 * dh
    assert in_cols == 2 * c_conv + 2 * qk_cols + v_cols
    cap = CAPACITY_FACTOR * s // n_exp
    c1 = 2 * c_conv
    c3 = c1 + 2 * qk_cols

    inv_freq = ROPE_THETA ** (-jnp.arange(0, dh, 2, dtype=F32) / dh)
    ang = positions.astype(F32).reshape(t, 1) * inv_freq
    cos_t = jnp.concatenate([jnp.cos(ang), jnp.cos(ang)], axis=-1)
    sin_t = jnp.concatenate([-jnp.sin(ang), jnp.sin(ang)], axis=-1)

    xf = x.reshape(t, d)
    tt = _tile(s, 128)
    nt = s // tt
    for l in range(depth):
        lam_init = 0.8 - 0.6 * math.exp(-0.3 * l)
        xn = _rmsnorm(xf, norm1_g[l], RMS_EPS, BF16)
        glu = _glu_mm(xn, w_in[l], c_conv, 0, c_conv, False, F32)
        qk = _rope_mm(xn, w_in[l], c1, 2 * qk_cols, qk_cols, cos_t, sin_t, dh ** -0.5 * LOG2_E, dh)
        v = _mm(xn, w_in[l], c3, v_cols, BF16)

        conv_out = _conv(glu.reshape(b, s, c_conv), conv_w[l], conv_b[l], conv_ln_g[l], conv_ln_b[l])
        attn = _attention(qk, v, lq1[l], lk1[l], lq2[l], lk2[l], subln_g[l], b, s, heads, dh, dv, lam_init)
        h = _out_mm(conv_out.reshape(t, c_conv), attn, w_out[l], xf)

        aff_t = _router(h, norm2_g[l], w_router[l].T)
        idx, gates, sel, tab = _topk(aff_t, b, s, cap, tt)
        idx_flat = idx.reshape(b, n_exp, cap).transpose(1, 0, 2).reshape(-1)
        gates_e = gates.reshape(b, n_exp, cap).transpose(1, 0, 2).reshape(n_exp, b * cap, 1)
        sel_t = sel.reshape(t, V7X_LANES)

        xs = _gather_norm(h, idx_flat, norm2_g[l], n_exp, b, cap)
        act = _expert_up(xs, w_gate[l], w_up[l])
        y = _down_mm(act, w_down[l], gates_e)
        xf = _combine(h, y.reshape(n_exp * b * cap, d // 2), idx_flat, tab.reshape(-1), sel_t, final_g,
                      n_exp, b, s, cap, final_norm=(l == depth - 1), tt=tt)
    return xf.reshape(b, s, d)
```
